```python
import math
import jax, jax.numpy as jnp
from jax import lax
import numpy as np

D_MODEL = 1024
BATCH = 8
SEQ = 2048
DEPTH = 4

HEAD_DIM = 64
ATTN_WIDTH = D_MODEL // 2
CONV_WIDTH = D_MODEL // 4
POOL_WIDTH = D_MODEL // 4
MIX_WIDTH = ATTN_WIDTH + CONV_WIDTH + POOL_WIDTH
N_Q_HEADS = ATTN_WIDTH // HEAD_DIM
N_KV_HEADS = 2
KV_WIDTH = N_KV_HEADS * HEAD_DIM
CONV_K = 3
POOL_WINDOWS = (2, 4, 8, 16)
POOL_GROUP = POOL_WIDTH // len(POOL_WINDOWS)
IN_WIDTH = ATTN_WIDTH + 2 * KV_WIDTH + 3 * CONV_WIDTH + POOL_WIDTH
D_FF = 4 * D_MODEL
WINDOW = 128
BLOCK = 128
N_BUCKETS = 32
MAX_DISTANCE = 128
EPS = 1e-6
NEG = -1e30

kernel_name = "hymba_style_conv_swa_pool_hybrid"


def rmsnorm(x, g):
    xf = x.astype(jnp.float32)
    y = xf * lax.rsqrt(jnp.mean(xf * xf, axis=-1, keepdims=True) + EPS)
    return (y * g.astype(jnp.float32)).astype(x.dtype)


def t5_causal_bucket(dist):
    n = jnp.maximum(dist, 0)
    max_exact = N_BUCKETS // 2
    nf = jnp.maximum(n, 1).astype(jnp.float32)
    large = max_exact + (jnp.log(nf / max_exact) / math.log(MAX_DISTANCE / max_exact)
                         * (N_BUCKETS - max_exact)).astype(jnp.int32)
    large = jnp.minimum(large, N_BUCKETS - 1)
    return jnp.where(n < max_exact, n, large)


def sliding_window_attention(q, k, v, sinks, rel_bias):
    B, S = q.shape[0], q.shape[1]
    nb = S // BLOCK
    G = N_Q_HEADS // N_KV_HEADS
    qb = q.reshape(B, nb, BLOCK, N_KV_HEADS, G, HEAD_DIM)
    kb = k.reshape(B, nb, BLOCK, N_KV_HEADS, HEAD_DIM)
    vb = v.reshape(B, nb, BLOCK, N_KV_HEADS, HEAD_DIM)
    pad = ((0, 0), (1, 0), (0, 0), (0, 0), (0, 0))
    k_band = jnp.concatenate([jnp.pad(kb, pad)[:, :-1], kb], axis=2)
    v_band = jnp.concatenate([jnp.pad(vb, pad)[:, :-1], vb], axis=2)

    scale = 1.0 / math.sqrt(HEAD_DIM)
    scores = jnp.einsum('bnqhgd,bnkhd->bnhgqk', qb, k_band).astype(jnp.float32) * scale

    qi = jnp.arange(BLOCK, dtype=jnp.int32)[:, None] + BLOCK
    kj = jnp.arange(2 * BLOCK, dtype=jnp.int32)[None, :]
    dist = qi - kj
    bias = rel_bias.astype(jnp.float32)[t5_causal_bucket(dist)]
    bias = jnp.transpose(bias, (2, 0, 1)).reshape(N_KV_HEADS, G, BLOCK, 2 * BLOCK)
    band_ok = (dist >= 0) & (dist < WINDOW)
    kpos = jnp.arange(nb, dtype=jnp.int32)[:, None] * BLOCK - BLOCK + kj
    valid = band_ok[None] & (kpos >= 0)[:, None, :]

    scores = jnp.where(valid[None, :, None, None], scores + bias, NEG)
    sink = sinks.astype(jnp.float32).reshape(1, 1, N_KV_HEADS, G, 1, 1)
    m = jnp.maximum(jnp.max(scores, axis=-1, keepdims=True), sink)
    p = jnp.exp(scores - m)
    denom = jnp.sum(p, axis=-1, keepdims=True) + jnp.exp(sink - m)
    probs = (p / denom).astype(v.dtype)
    out = jnp.einsum('bnhgqk,bnkhd->bnqhgd', probs, v_band)
    return out.reshape(B, S, ATTN_WIDTH)


def short_conv_mixer(b_gate, c_gate, hc, conv_w):
    u = c_gate * hc
    y = lax.conv_general_dilated(
        u, conv_w[:, None, :].astype(u.dtype), window_strides=(1,),
        padding=[(CONV_K - 1, 0)], dimension_numbers=('NWC', 'WIO', 'NWC'),
        feature_group_count=CONV_WIDTH)
    return b_gate * y


def multiscale_pool_mixer(p, pool_w, pool_scale):
    B, S = p.shape[0], p.shape[1]
    pf = p.astype(jnp.float32)
    cs = jnp.pad(jnp.cumsum(pf, axis=1), ((0, 0), (1, 0), (0, 0)))
    t = jnp.arange(S, dtype=jnp.int32)
    means = []
    for gi, w in enumerate(POOL_WINDOWS):
        csg = cs[:, :, gi * POOL_GROUP:(gi + 1) * POOL_GROUP]
        upper = csg[:, 1:]
        lower = jnp.pad(csg, ((0, 0), (w - 1, 0), (0, 0)))[:, :S]
        count = jnp.minimum(t + 1, w).astype(jnp.float32)[None, :, None]
        means.append((upper - lower) / count)
    pooled = jnp.concatenate(means, axis=-1) - pf
    pooled = pooled.reshape(B, S, len(POOL_WINDOWS), POOL_GROUP)
    mixed = jnp.einsum('bsgc,gcd->bsgd', pooled, pool_w.astype(jnp.float32)).reshape(B, S, POOL_WIDTH)
    return (mixed * pool_scale.astype(jnp.float32)).astype(p.dtype)


def setup_inputs(seed: int = 0) -> dict:
    key = jax.random.key(seed)
    ks = jax.random.split(key, 14)
    f32 = jnp.float32
    nrm = lambda k, shape, s: jax.random.normal(k, shape, f32) * s
    return {
        "x": nrm(ks[0], (BATCH, SEQ, D_MODEL), 1.0),
        "norm1": 1.0 + nrm(ks[1], (DEPTH, D_MODEL), 0.02),
        "w_in": nrm(ks[2], (DEPTH, D_MODEL, IN_WIDTH), D_MODEL ** -0.5),
        "conv_w": nrm(ks[3], (DEPTH, CONV_K, CONV_WIDTH), CONV_K ** -0.5),
        "sinks": nrm(ks[4], (DEPTH, N_Q_HEADS), 0.5),
        "pool_w": nrm(ks[5], (DEPTH, len(POOL_WINDOWS), POOL_GROUP, POOL_GROUP), POOL_GROUP ** -0.5),
        "pool_scale": 1.0 + nrm(ks[6], (DEPTH, POOL_WIDTH), 0.02),
        "w_out": nrm(ks[7], (DEPTH, MIX_WIDTH, D_MODEL), MIX_WIDTH ** -0.5),
        "norm2": 1.0 + nrm(ks[8], (DEPTH, D_MODEL), 0.02),
        "w1": nrm(ks[9], (DEPTH, D_MODEL, D_FF), D_MODEL ** -0.5),
        "w2": nrm(ks[10], (DEPTH, D_FF, D_MODEL), D_FF ** -0.5),
        "rel_bias": nrm(ks[11], (N_BUCKETS, N_Q_HEADS), 0.2),
        "final_norm": 1.0 + nrm(ks[12], (D_MODEL,), 0.02),
    }


def reference(x, norm1, w_in, conv_w, sinks, pool_w, pool_scale, w_out, norm2, w1, w2, rel_bias, final_norm):
    splits = np.cumsum([ATTN_WIDTH, KV_WIDTH, KV_WIDTH, CONV_WIDTH, CONV_WIDTH, CONV_WIDTH]).tolist()
    for l in range(DEPTH):
        h = rmsnorm(x, norm1[l])
        proj = h @ w_in[l]
        q, k, v, b_gate, c_gate, hc, p = jnp.split(proj, splits, axis=-1)
        attn_out = sliding_window_attention(q, k, v, sinks[l], rel_bias)
        conv_out = short_conv_mixer(b_gate, c_gate, hc, conv_w[l])
        pool_out = multiscale_pool_mixer(p, pool_w[l], pool_scale[l])
        mixed = jnp.concatenate([attn_out, conv_out, pool_out], axis=-1)
        x = x + mixed @ w_out[l]
        h2 = rmsnorm(x, norm2[l])
        x = x + jnp.square(jax.nn.relu(h2 @ w1[l])) @ w2[l]
    return rmsnorm(x, final_norm)
```

```python
import functools
import math

import numpy as np
import jax
import jax.numpy as jnp
from jax import lax
from jax.experimental import pallas as pl
from jax.experimental.pallas import tpu as pltpu

D_MODEL = 1024
HEAD_DIM = 64
ATTN_WIDTH = 512
CONV_WIDTH = 256
POOL_WIDTH = 256
N_Q_HEADS = 8
N_KV_HEADS = 2
GROUP = N_Q_HEADS // N_KV_HEADS
KV_WIDTH = N_KV_HEADS * HEAD_DIM
POOL_WINDOWS = (2, 4, 8, 16)
POOL_GROUP = 64
IN_WIDTH = 1792
D_FF = 4096
WINDOW = 128
BLOCK = 128
N_BUCKETS = 32
MAX_DISTANCE = 128
EPS = 1e-6
NEG = -1e30

_QKV_END = ATTN_WIDTH + 2 * KV_WIDTH
_GATES_WIDTH = IN_WIDTH - _QKV_END

TM = 512
KV_HALO = BLOCK
CONV_HALO = 8
POOL_HALO = 32
FF_CHUNK = 1024
V7X_VMEM_LIMIT_BYTES = 56 * 1024 * 1024

_F32 = jnp.float32
_BF16 = jnp.bfloat16


def _bucket_table():
    kj = np.arange(2 * BLOCK, dtype=np.int32)[:, None]
    qi = np.arange(BLOCK, dtype=np.int32)[None, :] + BLOCK
    dist = qi - kj
    n = np.maximum(dist, 0)
    max_exact = N_BUCKETS // 2
    nf = np.maximum(n, 1).astype(np.float32)
    large = max_exact + (np.log(nf / np.float32(max_exact)) / np.float32(math.log(MAX_DISTANCE / max_exact))
                         * np.float32(N_BUCKETS - max_exact)).astype(np.int32)
    large = np.minimum(large, N_BUCKETS - 1)
    bucket = np.where(n < max_exact, n, large)
    valid = (dist >= 0) & (dist < WINDOW)
    return np.where(valid, bucket, -1).astype(np.int32)


def _rmsnorm(x, g):
    ms = jnp.mean(x * x, axis=-1, keepdims=True)
    return x * lax.rsqrt(ms + EPS) * g


def _build_tables(relb_ref, poolw_ref, bkt_ref, bias_scr, wbd_scr):
    bkt = bkt_ref[...]
    for h in range(N_Q_HEADS):
        acc = jnp.where(bkt < 0, NEG, 0.0).astype(_F32)
        for b in range(N_BUCKETS):
            acc = jnp.where(bkt == b, relb_ref[b, h], acc)
        g, gi = divmod(h, GROUP)
        bias_scr[g, :, gi * BLOCK:(gi + 1) * BLOCK] = acc
    rows = lax.broadcasted_iota(jnp.int32, (POOL_GROUP, POOL_WIDTH), 0)
    cols = lax.broadcasted_iota(jnp.int32, (POOL_GROUP, POOL_WIDTH), 1)
    rep = jnp.where((cols % POOL_GROUP) == rows, 1.0, 0.0).astype(_BF16)
    tiled = jnp.dot(poolw_ref[...].astype(_BF16), rep, preferred_element_type=_F32)
    r2 = lax.broadcasted_iota(jnp.int32, (POOL_WIDTH, POOL_WIDTH), 0) // POOL_GROUP
    c2 = lax.broadcasted_iota(jnp.int32, (POOL_WIDTH, POOL_WIDTH), 1) // POOL_GROUP
    wbd_scr[...] = jnp.where(r2 == c2, tiled, 0.0).astype(_BF16)


def _attention_block(n, first_tile, q_scr, k_scr, v_scr, bias_scr, sink_rows, mixed_scr):
    r0 = n * BLOCK
    kband = k_scr[r0:r0 + 2 * BLOCK, :]
    v_t = v_scr[r0:r0 + 2 * BLOCK, :].T.astype(_BF16)
    zeros = jnp.zeros((HEAD_DIM, GROUP * BLOCK), _F32)
    scale = 1.0 / math.sqrt(HEAD_DIM)
    for g in range(N_KV_HEADS):
        qg_t = (q_scr[r0:r0 + BLOCK, g * GROUP * HEAD_DIM:(g + 1) * GROUP * HEAD_DIM] * scale).T
        qcat = jnp.concatenate([qg_t[gi * HEAD_DIM:(gi + 1) * HEAD_DIM, :] for gi in range(GROUP)], axis=1)
        rhs = jnp.concatenate([qcat, zeros] if g == 0 else [zeros, qcat], axis=0).astype(_BF16)
        s_t = jnp.dot(kband, rhs, preferred_element_type=_F32) + bias_scr[g]
        if n == 0:
            s_t = jnp.concatenate([s_t[:BLOCK] + jnp.where(first_tile, NEG, 0.0), s_t[BLOCK:]], axis=0)
        sink = sink_rows[g]
        m = jnp.maximum(jnp.max(s_t, axis=0, keepdims=True), sink)
        e = jnp.exp(s_t - m)
        denom = jnp.sum(e, axis=0, keepdims=True) + jnp.exp(sink - m)
        o_t = jnp.dot(v_t[g * HEAD_DIM:(g + 1) * HEAD_DIM, :], e.astype(_BF16), preferred_element_type=_F32)
        o_t = o_t * (1.0 / denom)
        o4 = jnp.concatenate([o_t[:, gi * BLOCK:(gi + 1) * BLOCK] for gi in range(GROUP)], axis=0)
        mixed_scr[r0:r0 + BLOCK, g * GROUP * HEAD_DIM:(g + 1) * GROUP * HEAD_DIM] = o4.T.astype(_BF16)


def _layer_kernel(x_ref, n1_ref, win_ref, convw_ref, sinks_ref, relb_ref, poolw_ref, pscale_ref, wout_ref,
                  n2_ref, w1_ref, w2_ref, bkt_ref, *rest, final):
    if final:
        fnorm_ref, o_ref, *scratch = rest
    else:
        o_ref, *scratch = rest
    (q_scr, k_scr, v_scr, g_scr, u_scr, p_scr, a1_scr, a2_scr, a3_scr, mixed_scr, bias_scr, wbd_scr) = scratch
    b = pl.program_id(0)
    j = pl.program_id(1)
    first_tile = j == 0

    @pl.when(jnp.logical_and(b == 0, first_tile))
    def _():
        _build_tables(relb_ref, poolw_ref, bkt_ref, bias_scr, wbd_scr)

    @pl.when(first_tile)
    def _():
        k_scr[0:KV_HALO, :] = jnp.zeros((KV_HALO, KV_WIDTH), _BF16)
        v_scr[0:KV_HALO, :] = jnp.zeros((KV_HALO, KV_WIDTH), _F32)
        u_scr[0:CONV_HALO, :] = jnp.zeros((CONV_HALO, CONV_WIDTH), _F32)
        p_scr[0:POOL_HALO, :] = jnp.zeros((POOL_HALO, POOL_WIDTH), _F32)

    x = x_ref[0]
    h = _rmsnorm(x, n1_ref[...]).astype(_BF16)
    qkv = jnp.dot(h, win_ref[:, 0:_QKV_END], preferred_element_type=_F32)
    q_scr[...] = qkv[:, 0:ATTN_WIDTH]
    k_scr[KV_HALO:KV_HALO + TM, :] = qkv[:, ATTN_WIDTH:ATTN_WIDTH + KV_WIDTH].astype(_BF16)
    v_scr[KV_HALO:KV_HALO + TM, :] = qkv[:, ATTN_WIDTH + KV_WIDTH:_QKV_END]
    g_scr[...] = jnp.dot(h, win_ref[:, _QKV_END:IN_WIDTH], preferred_element_type=_F32)

    lane = lax.broadcasted_iota(jnp.int32, (1, GROUP * BLOCK), 1)
    sink_rows = []
    for g in range(N_KV_HEADS):
        row = jnp.full((1, GROUP * BLOCK), sinks_ref[g * GROUP + GROUP - 1], _F32)
        for gi in range(GROUP - 2, -1, -1):
            row = jnp.where(lane < (gi + 1) * BLOCK, sinks_ref[g * GROUP + gi], row)
        sink_rows.append(row)
    for n in range(TM // BLOCK):
        _attention_block(n, first_tile, q_scr, k_scr, v_scr, bias_scr, sink_rows, mixed_scr)
    k_scr[0:KV_HALO, :] = k_scr[TM:TM + KV_HALO, :]
    v_scr[0:KV_HALO, :] = v_scr[TM:TM + KV_HALO, :]

    c0 = CONV_HALO
    u_scr[c0:c0 + TM, :] = g_scr[:, CONV_WIDTH:2 * CONV_WIDTH] * g_scr[:, 2 * CONV_WIDTH:3 * CONV_WIDTH]
    cw = convw_ref[...]
    y = (cw[0:1, :] * u_scr[c0 - 2:c0 - 2 + TM, :] + cw[1:2, :] * u_scr[c0 - 1:c0 - 1 + TM, :]
         + cw[2:3, :] * u_scr[c0:c0 + TM, :])
    mixed_scr[:, ATTN_WIDTH:ATTN_WIDTH + CONV_WIDTH] = (g_scr[:, 0:CONV_WIDTH] * y).astype(_BF16)
    u_scr[0:CONV_HALO, :] = u_scr[TM:TM + CONV_HALO, :]

    p0 = POOL_HALO
    pe = p0 + TM
    p = g_scr[:, 3 * CONV_WIDTH:_GATES_WIDTH]
    p_scr[p0:pe, :] = p
    a1_scr[8:pe, :] = p_scr[8:pe, :] + p_scr[7:pe - 1, :]
    a2_scr[16:pe, :] = a1_scr[16:pe, :] + a1_scr[14:pe - 2, :]
    a3_scr[24:pe, :] = a2_scr[24:pe, :] + a2_scr[20:pe - 4, :]
    s16 = a3_scr[p0:pe, :] + a3_scr[p0 - 8:pe - 8, :]
    plane = lax.broadcasted_iota(jnp.int32, (TM, POOL_WIDTH), 1)
    ssum = jnp.where(plane < POOL_GROUP, a1_scr[p0:pe, :],
                     jnp.where(plane < 2 * POOL_GROUP, a2_scr[p0:pe, :],
                               jnp.where(plane < 3 * POOL_GROUP, a3_scr[p0:pe, :], s16)))
    win = jnp.where(plane < POOL_GROUP, POOL_WINDOWS[0],
                    jnp.where(plane < 2 * POOL_GROUP, POOL_WINDOWS[1],
                              jnp.where(plane < 3 * POOL_GROUP, POOL_WINDOWS[2], POOL_WINDOWS[3])))
    t = j * TM + lax.broadcasted_iota(jnp.int32, (TM, POOL_WIDTH), 0)
    count = jnp.minimum(t + 1, win).astype(_F32)
    pooled = ssum / count - p
    mixed_pool = jnp.dot(pooled.astype(_BF16), wbd_scr[...], preferred_element_type=_F32) * pscale_ref[...]
    mixed_scr[:, ATTN_WIDTH + CONV_WIDTH:D_MODEL] = mixed_pool.astype(_BF16)
    p_scr[0:POOL_HALO, :] = p_scr[TM:TM + POOL_HALO, :]

    x1 = x + jnp.dot(mixed_scr[...], wout_ref[...], preferred_element_type=_F32)
    h2 = _rmsnorm(x1, n2_ref[...]).astype(_BF16)
    acc = x1
    for c in range(D_FF // FF_CHUNK):
        hid = jnp.dot(h2, w1_ref[:, c * FF_CHUNK:(c + 1) * FF_CHUNK], preferred_element_type=_F32)
        act = jnp.square(jnp.maximum(hid, 0.0)).astype(_BF16)
        acc = acc + jnp.dot(act, w2_ref[c * FF_CHUNK:(c + 1) * FF_CHUNK, :], preferred_element_type=_F32)
    if final:
        acc = _rmsnorm(acc, fnorm_ref[...])
    o_ref[0] = acc


def _resident(shape):
    return pl.BlockSpec(shape, lambda b, j: (0,) * len(shape), pipeline_mode=pl.Buffered(1))


def _layer_call(x, n1, win, convw, sinks, relb, poolw, pscale, wout, n2, w1, w2, bkt, fnorm):
    batch, seq, _ = x.shape
    assert seq % TM == 0 and TM % BLOCK == 0
    final = fnorm is not None
    smem = pl.BlockSpec(memory_space=pltpu.SMEM)
    tile = pl.BlockSpec((1, TM, D_MODEL), lambda b, j: (b, j, 0))
    in_specs = [
        tile,
        _resident((1, D_MODEL)),
        _resident((D_MODEL, IN_WIDTH)),
        _resident((3, CONV_WIDTH)),
        smem,
        smem,
        _resident((POOL_WIDTH, POOL_GROUP)),
        _resident((1, POOL_WIDTH)),
        _resident((D_MODEL, D_MODEL)),
        _resident((1, D_MODEL)),
        _resident((D_MODEL, D_FF)),
        _resident((D_FF, D_MODEL)),
        _resident((2 * BLOCK, BLOCK)),
    ]
    args = [x, n1, win, convw, sinks, relb, poolw, pscale, wout, n2, w1, w2, bkt]
    if final:
        in_specs.append(_resident((1, D_MODEL)))
        args.append(fnorm)
    scratch = [
        pltpu.VMEM((TM, ATTN_WIDTH), _F32),
        pltpu.VMEM((KV_HALO + TM, KV_WIDTH), _BF16),
        pltpu.VMEM((KV_HALO + TM, KV_WIDTH), _F32),
        pltpu.VMEM((TM, _GATES_WIDTH), _F32),
        pltpu.VMEM((CONV_HALO + TM, CONV_WIDTH), _F32),
        pltpu.VMEM((POOL_HALO + TM, POOL_WIDTH), _F32),
        pltpu.VMEM((POOL_HALO + TM, POOL_WIDTH), _F32),
        pltpu.VMEM((POOL_HALO + TM, POOL_WIDTH), _F32),
        pltpu.VMEM((POOL_HALO + TM, POOL_WIDTH), _F32),
        pltpu.VMEM((TM, D_MODEL), _BF16),
        pltpu.VMEM((N_KV_HEADS, 2 * BLOCK, GROUP * BLOCK), _F32),
        pltpu.VMEM((POOL_WIDTH, POOL_WIDTH), _BF16),
    ]
    return pl.pallas_call(
        functools.partial(_layer_kernel, final=final),
        grid=(batch, seq // TM),
        in_specs=in_specs,
        out_specs=tile,
        out_shape=jax.ShapeDtypeStruct(x.shape, x.dtype),
        scratch_shapes=scratch,
        compiler_params=pltpu.CompilerParams(
            dimension_semantics=("arbitrary", "arbitrary"),
            vmem_limit_bytes=V7X_VMEM_LIMIT_BYTES,
        ),
        name="hybrid_layer_final" if final else "hybrid_layer",
    )(*args)


def kernel(x, norm1, w_in, conv_w, sinks, pool_w, pool_scale, w_out, norm2, w1, w2, rel_bias, final_norm):
    depth = w_in.shape[0]
    bkt = jnp.asarray(_bucket_table())
    for l in range(depth):
        fnorm = final_norm.reshape(1, D_MODEL) if l == depth - 1 else None
        x = _layer_call(
            x,
            norm1[l].reshape(1, D_MODEL),
            w_in[l].astype(_BF16),
            conv_w[l],
            sinks[l],
            rel_bias,
            pool_w[l].reshape(POOL_WIDTH, POOL_GROUP),
            pool_scale[l].reshape(1, POOL_WIDTH),
            w_out[l].astype(_BF16),
            norm2[l].reshape(1, D_MODEL),
            w1[l].astype(_BF16),
            w2[l].astype(_BF16),
            bkt,
            fnorm,
        )
    return x
```

```python
import functools
import math

import numpy as np
import jax
import jax.numpy as jnp
from jax import lax
from jax.experimental import pallas as pl
from jax.experimental.pallas import tpu as pltpu

D_MODEL = 1024
HEAD_DIM = 64
ATTN_WIDTH = 512
CONV_WIDTH = 256
CONV_K = 3
POOL_WIDTH = 256
N_Q_HEADS = 8
N_KV_HEADS = 2
GROUP = N_Q_HEADS // N_KV_HEADS
KV_WIDTH = N_KV_HEADS * HEAD_DIM
POOL_WINDOWS = (2, 4, 8, 16)
POOL_GROUP = 64
IN_WIDTH = 1792
D_FF = 4096
WINDOW = 128
BLOCK = 128
N_BUCKETS = 32
MAX_DISTANCE = 128
EPS = 1e-6
NEG = -1e30

_QKV_END = ATTN_WIDTH + 2 * KV_WIDTH
_GATES_WIDTH = IN_WIDTH - _QKV_END

TM = 512
KV_HALO = BLOCK
CONV_HALO = 8
POOL_HALO = 32
FF_CHUNK = 1024
V7X_VMEM_LIMIT_BYTES = 60 * 1024 * 1024
_PHASE_ORDER = "m m x m x m x m x m x x x m x m x m x m x m m x m m x m m m".split()

_F32 = jnp.float32
_BF16 = jnp.bfloat16


def _bucket_table():
    kj = np.arange(2 * BLOCK, dtype=np.int32)[:, None]
    qi = np.arange(BLOCK, dtype=np.int32)[None, :] + BLOCK
    dist = qi - kj
    n = np.maximum(dist, 0)
    max_exact = N_BUCKETS // 2
    nf = np.maximum(n, 1).astype(np.float32)
    large = max_exact + (np.log(nf / np.float32(max_exact)) / np.float32(math.log(MAX_DISTANCE / max_exact))
                         * np.float32(N_BUCKETS - max_exact)).astype(np.int32)
    large = np.minimum(large, N_BUCKETS - 1)
    bucket = np.where(n < max_exact, n, large)
    valid = (dist >= 0) & (dist < WINDOW)
    return np.where(valid, bucket, -1).astype(np.int32)


def _rmsnorm(x, g):
    ms = jnp.mean(x * x, axis=-1, keepdims=True)
    return x * lax.rsqrt(ms + EPS) * g


def _build_tables(relb_ref, poolw_ref, bkt_ref, bias_scr, wbd_scr):
    bkt = bkt_ref[...]
    for h in range(N_Q_HEADS):
        acc = jnp.where(bkt < 0, NEG, 0.0).astype(_F32)
        for b in range(N_BUCKETS):
            acc = jnp.where(bkt == b, relb_ref[b, h], acc)
        g, gi = divmod(h, GROUP)
        bias_scr[g, :, gi * BLOCK:(gi + 1) * BLOCK] = acc
    rows = lax.broadcasted_iota(jnp.int32, (POOL_GROUP, POOL_WIDTH), 0)
    cols = lax.broadcasted_iota(jnp.int32, (POOL_GROUP, POOL_WIDTH), 1)
    rep = jnp.where((cols % POOL_GROUP) == rows, 1.0, 0.0).astype(_BF16)
    tiled = jnp.dot(poolw_ref[...].astype(_BF16), rep, preferred_element_type=_F32)
    r2 = lax.broadcasted_iota(jnp.int32, (POOL_WIDTH, POOL_WIDTH), 0) // POOL_GROUP
    c2 = lax.broadcasted_iota(jnp.int32, (POOL_WIDTH, POOL_WIDTH), 1) // POOL_GROUP
    wbd_scr[...] = jnp.where(r2 == c2, tiled, 0.0).astype(_BF16)


def _attention_scores(n, g, first_tile, q_scr, k_scr, bias_scr, sink_row):
    r0 = n * BLOCK
    kband = k_scr[r0:r0 + 2 * BLOCK, :]
    zeros = jnp.zeros((HEAD_DIM, GROUP * BLOCK), _F32)
    scale = 1.0 / math.sqrt(HEAD_DIM)
    qg_t = (q_scr[r0:r0 + BLOCK, g * GROUP * HEAD_DIM:(g + 1) * GROUP * HEAD_DIM] * scale).T
    qcat = jnp.concatenate([qg_t[gi * HEAD_DIM:(gi + 1) * HEAD_DIM, :] for gi in range(GROUP)], axis=1)
    rhs = jnp.concatenate([qcat, zeros] if g == 0 else [zeros, qcat], axis=0).astype(_BF16)
    s_t = jnp.dot(kband, rhs, preferred_element_type=_F32) + bias_scr[g]
    if n == 0:
        s_t = jnp.concatenate([s_t[:BLOCK] + jnp.where(first_tile, NEG, 0.0), s_t[BLOCK:]], axis=0)
    m = jnp.maximum(jnp.max(s_t, axis=0, keepdims=True), sink_row)
    e = jnp.exp(s_t - m)
    denom = jnp.sum(e, axis=0, keepdims=True) + jnp.exp(sink_row - m)
    return e.astype(_BF16), 1.0 / denom


def _attention_output(n, g, e, inv_denom, v_scr, mixed_scr):
    r0 = n * BLOCK
    v_t = v_scr[r0:r0 + 2 * BLOCK, :].T[g * HEAD_DIM:(g + 1) * HEAD_DIM, :].astype(_BF16)
    o_t = jnp.dot(v_t, e, preferred_element_type=_F32) * inv_denom
    o4 = jnp.concatenate([o_t[:, gi * BLOCK:(gi + 1) * BLOCK] for gi in range(GROUP)], axis=0)
    mixed_scr[r0:r0 + BLOCK, g * GROUP * HEAD_DIM:(g + 1) * GROUP * HEAD_DIM] = o4.T.astype(_BF16)


def _mlp_half(x1_ref, h2_ref, w1_ref, w2_ref, fnorm_ref, o_ref):
    half = FF_CHUNK // 2
    halfd = D_MODEL // 2
    acc = [None, None]
    for c in range(D_FF // FF_CHUNK):
        acts = []
        for i in range(2):
            lo = c * FF_CHUNK + i * half
            hid = jnp.dot(h2_ref[...], w1_ref[:, lo:lo + half], preferred_element_type=_F32)
            acts.append(jnp.square(jnp.maximum(hid, 0.0)).astype(_BF16))
            yield
        act = jnp.concatenate(acts, axis=1)
        for i in range(2):
            part = jnp.dot(act, w2_ref[c * FF_CHUNK:(c + 1) * FF_CHUNK, i * halfd:(i + 1) * halfd],
                           preferred_element_type=_F32)
            acc[i] = part if acc[i] is None else acc[i] + part
            yield
    out = x1_ref[...] + jnp.concatenate(acc, axis=1)
    if fnorm_ref is not None:
        out = _rmsnorm(out, fnorm_ref[...])
    o_ref[0] = out
    yield


def _mixer_half(x_ref, x1_ref, h2_ref, seq_tile, n1_ref, win_ref, convw_ref, sinks_ref, pscale_ref, wout_ref,
                n2_ref, layer, q_scr, k_scr, v_scr, g_scr, u_scr, p_scr, a1_scr, a2_scr, a3_scr, mixed_scr,
                bias_scr, wbd_scr):
    first_tile = seq_tile == 0

    x = x_ref[0]
    h = _rmsnorm(x, n1_ref[...]).astype(_BF16)
    q_scr[...] = jnp.dot(h, win_ref[:, 0:ATTN_WIDTH], preferred_element_type=_F32)
    for r in range(0, TM, TM // 2):
        kv = jnp.dot(h[r:r + TM // 2], win_ref[:, ATTN_WIDTH:_QKV_END], preferred_element_type=_F32)
        k_scr[KV_HALO + r:KV_HALO + r + TM // 2, :] = kv[:, 0:KV_WIDTH].astype(_BF16)
        v_scr[KV_HALO + r:KV_HALO + r + TM // 2, :] = kv[:, KV_WIDTH:2 * KV_WIDTH]
    yield

    lane = lax.broadcasted_iota(jnp.int32, (1, GROUP * BLOCK), 1)
    sink_rows = []
    for g in range(N_KV_HEADS):
        row = jnp.full((1, GROUP * BLOCK), sinks_ref[layer, g * GROUP + GROUP - 1], _F32)
        for gi in range(GROUP - 2, -1, -1):
            row = jnp.where(lane < (gi + 1) * BLOCK, sinks_ref[layer, g * GROUP + gi], row)
        sink_rows.append(row)
    gates_half = _GATES_WIDTH // 2
    n_pairs = (TM // BLOCK) * N_KV_HEADS
    pending = None
    for k in range(n_pairs):
        n, g = divmod(k, N_KV_HEADS)
        if pending is not None:
            _attention_output(*pending, v_scr, mixed_scr)
        e, inv_denom = _attention_scores(n, g, first_tile, q_scr, k_scr, bias_scr, sink_rows[g])
        pending = (n, g, e, inv_denom)
        yield
        if k == n_pairs // 2 - 1:
            g_scr[:, 0:gates_half] = jnp.dot(h, win_ref[:, _QKV_END:_QKV_END + gates_half],
                                             preferred_element_type=_F32)
            yield
    _attention_output(*pending, v_scr, mixed_scr)
    g_scr[:, gates_half:_GATES_WIDTH] = jnp.dot(h, win_ref[:, _QKV_END + gates_half:IN_WIDTH],
                                                preferred_element_type=_F32)
    yield
    k_scr[0:KV_HALO, :] = k_scr[TM:TM + KV_HALO, :]
    v_scr[0:KV_HALO, :] = v_scr[TM:TM + KV_HALO, :]

    c0 = CONV_HALO
    u_scr[c0:c0 + TM, :] = g_scr[:, CONV_WIDTH:2 * CONV_WIDTH] * g_scr[:, 2 * CONV_WIDTH:3 * CONV_WIDTH]
    cw = convw_ref[...]
    y = (cw[0:1, :] * u_scr[c0 - 2:c0 - 2 + TM, :] + cw[1:2, :] * u_scr[c0 - 1:c0 - 1 + TM, :]
         + cw[2:3, :] * u_scr[c0:c0 + TM, :])
    mixed_scr[:, ATTN_WIDTH:ATTN_WIDTH + CONV_WIDTH] = (g_scr[:, 0:CONV_WIDTH] * y).astype(_BF16)
    u_scr[0:CONV_HALO, :] = u_scr[TM:TM + CONV_HALO, :]

    p0 = POOL_HALO
    pe = p0 + TM
    p = g_scr[:, 3 * CONV_WIDTH:_GATES_WIDTH]
    p_scr[p0:pe, :] = p
    a1_scr[8:pe, :] = p_scr[8:pe, :] + p_scr[7:pe - 1, :]
    a2_scr[16:pe, :] = a1_scr[16:pe, :] + a1_scr[14:pe - 2, :]
    a3_scr[24:pe, :] = a2_scr[24:pe, :] + a2_scr[20:pe - 4, :]
    s16 = a3_scr[p0:pe, :] + a3_scr[p0 - 8:pe - 8, :]
    plane = lax.broadcasted_iota(jnp.int32, (TM, POOL_WIDTH), 1)
    ssum = jnp.where(plane < POOL_GROUP, a1_scr[p0:pe, :],
                     jnp.where(plane < 2 * POOL_GROUP, a2_scr[p0:pe, :],
                               jnp.where(plane < 3 * POOL_GROUP, a3_scr[p0:pe, :], s16)))
    win = jnp.where(plane < POOL_GROUP, POOL_WINDOWS[0],
                    jnp.where(plane < 2 * POOL_GROUP, POOL_WINDOWS[1],
                              jnp.where(plane < 3 * POOL_GROUP, POOL_WINDOWS[2], POOL_WINDOWS[3])))
    t = seq_tile * TM + lax.broadcasted_iota(jnp.int32, (TM, POOL_WIDTH), 0)
    count = jnp.minimum(t + 1, win).astype(_F32)
    pooled = (ssum / count - p).astype(_BF16)
    for r in range(0, TM, TM // 2):
        mixed_pool = jnp.dot(pooled[r:r + TM // 2], wbd_scr[...], preferred_element_type=_F32) * pscale_ref[...]
        mixed_scr[r:r + TM // 2, ATTN_WIDTH + CONV_WIDTH:D_MODEL] = mixed_pool.astype(_BF16)
    p_scr[0:POOL_HALO, :] = p_scr[TM:TM + POOL_HALO, :]
    yield

    x1 = x + jnp.dot(mixed_scr[...], wout_ref[...], preferred_element_type=_F32)
    x1_ref[...] = x1
    h2_ref[...] = _rmsnorm(x1, n2_ref[...]).astype(_BF16)
    yield


def _layer_kernel(x_ref, n1_ref, win_ref, convw_ref, sinks_ref, relb_ref, poolw_ref, pscale_ref, wout_ref,
                  n2_ref, w1_ref, w2_ref, bkt_ref, *rest, layer, final, tiles_per_seq):
    if final:
        fnorm_ref, o_ref, *scratch = rest
    else:
        o_ref, *scratch = rest
    (x1_scr, h2_scr, q_scr, k_scr, v_scr, g_scr, u_scr, p_scr, a1_scr, a2_scr, a3_scr, mixed_scr, bias_scr,
     wbd_scr) = scratch
    s = pl.program_id(0)
    n_tiles = pl.num_programs(0) - 1
    slot = s % 2
    seq_tile = jnp.minimum(s, n_tiles - 1) % tiles_per_seq

    @pl.when(s == 0)
    def _():
        _build_tables(relb_ref, poolw_ref, bkt_ref, bias_scr, wbd_scr)
        x1_scr[1] = jnp.zeros((TM, D_MODEL), _F32)
        h2_scr[1] = jnp.zeros((TM, D_MODEL), _BF16)

    @pl.when(seq_tile == 0)
    def _():
        k_scr[0:KV_HALO, :] = jnp.zeros((KV_HALO, KV_WIDTH), _BF16)
        v_scr[0:KV_HALO, :] = jnp.zeros((KV_HALO, KV_WIDTH), _F32)
        u_scr[0:CONV_HALO, :] = jnp.zeros((CONV_HALO, CONV_WIDTH), _F32)
        p_scr[0:POOL_HALO, :] = jnp.zeros((POOL_HALO, POOL_WIDTH), _F32)

    mlp = _mlp_half(x1_scr.at[1 - slot], h2_scr.at[1 - slot], w1_ref, w2_ref, fnorm_ref if final else None, o_ref)
    mixer = _mixer_half(x_ref, x1_scr.at[slot], h2_scr.at[slot], seq_tile, n1_ref, win_ref, convw_ref, sinks_ref,
                        pscale_ref, wout_ref, n2_ref, layer, q_scr, k_scr, v_scr, g_scr, u_scr, p_scr, a1_scr,
                        a2_scr, a3_scr, mixed_scr, bias_scr, wbd_scr)
    for who in _PHASE_ORDER:
        next(mlp if who == "m" else mixer)
    for half in (mlp, mixer):
        assert next(half, "done") == "done", "phase order does not cover every phase"


def _resident(shape, layer=None):
    if layer is None:
        return pl.BlockSpec(shape, lambda s: (0,) * len(shape), pipeline_mode=pl.Buffered(1))
    return pl.BlockSpec((None,) + shape, lambda s: (layer,) + (0,) * len(shape), pipeline_mode=pl.Buffered(1))


def _layer_call(x, layer, n1, win, convw, sinks, relb, poolw, pscale, wout, n2, w1, w2, bkt, fnorm):
    batch, seq, _ = x.shape
    assert seq % TM == 0 and TM % BLOCK == 0
    tiles_per_seq = seq // TM
    n_tiles = batch * tiles_per_seq
    final = fnorm is not None
    smem = pl.BlockSpec(memory_space=pltpu.SMEM)

    def in_tile(s):
        t = jnp.minimum(s, n_tiles - 1)
        return (t // tiles_per_seq, t % tiles_per_seq, 0)

    def out_tile(s):
        t = jnp.maximum(s - 1, 0)
        return (t // tiles_per_seq, t % tiles_per_seq, 0)

    in_specs = [
        pl.BlockSpec((1, TM, D_MODEL), in_tile),
        _resident((1, D_MODEL), layer),
        _resident((D_MODEL, IN_WIDTH), layer),
        _resident((CONV_K, CONV_WIDTH), layer),
        smem,
        smem,
        _resident((POOL_WIDTH, POOL_GROUP), layer),
        _resident((1, POOL_WIDTH), layer),
        _resident((D_MODEL, D_MODEL), layer),
        _resident((1, D_MODEL), layer),
        _resident((D_MODEL, D_FF), layer),
        _resident((D_FF, D_MODEL), layer),
        _resident((2 * BLOCK, BLOCK)),
    ]
    args = [x, n1, win, convw, sinks, relb, poolw, pscale, wout, n2, w1, w2, bkt]
    if final:
        in_specs.append(_resident((1, D_MODEL)))
        args.append(fnorm)
    scratch = [
        pltpu.VMEM((2, TM, D_MODEL), _F32),
        pltpu.VMEM((2, TM, D_MODEL), _BF16),
        pltpu.VMEM((TM, ATTN_WIDTH), _F32),
        pltpu.VMEM((KV_HALO + TM, KV_WIDTH), _BF16),
        pltpu.VMEM((KV_HALO + TM, KV_WIDTH), _F32),
        pltpu.VMEM((TM, _GATES_WIDTH), _F32),
        pltpu.VMEM((CONV_HALO + TM, CONV_WIDTH), _F32),
        pltpu.VMEM((POOL_HALO + TM, POOL_WIDTH), _F32),
        pltpu.VMEM((POOL_HALO + TM, POOL_WIDTH), _F32),
        pltpu.VMEM((POOL_HALO + TM, POOL_WIDTH), _F32),
        pltpu.VMEM((POOL_HALO + TM, POOL_WIDTH), _F32),
        pltpu.VMEM((TM, D_MODEL), _BF16),
        pltpu.VMEM((N_KV_HEADS, 2 * BLOCK, GROUP * BLOCK), _F32),
        pltpu.VMEM((POOL_WIDTH, POOL_WIDTH), _BF16),
    ]
    return pl.pallas_call(
        functools.partial(_layer_kernel, layer=layer, final=final, tiles_per_seq=tiles_per_seq),
        grid=(n_tiles + 1,),
        in_specs=in_specs,
        out_specs=pl.BlockSpec((1, TM, D_MODEL), out_tile),
        out_shape=jax.ShapeDtypeStruct(x.shape, x.dtype),
        scratch_shapes=scratch,
        compiler_params=pltpu.CompilerParams(
            dimension_semantics=("arbitrary",),
            vmem_limit_bytes=V7X_VMEM_LIMIT_BYTES,
        ),
        name="hybrid_layer_final" if final else "hybrid_layer",
    )(*args)


def kernel(x, norm1, w_in, conv_w, sinks, pool_w, pool_scale, w_out, norm2, w1, w2, rel_bias, final_norm):
    depth = w_in.shape[0]
    bkt = jnp.asarray(_bucket_table())
    n1 = norm1.reshape(depth, 1, D_MODEL)
    n2 = norm2.reshape(depth, 1, D_MODEL)
    win = w_in.astype(_BF16)
    wout = w_out.astype(_BF16)
    w1b = w1.astype(_BF16)
    w2b = w2.astype(_BF16)
    poolw = pool_w.reshape(depth, POOL_WIDTH, POOL_GROUP)
    pscale = pool_scale.reshape(depth, 1, POOL_WIDTH)
    for l in range(depth):
        fnorm = final_norm.reshape(1, D_MODEL) if l == depth - 1 else None
        x = _layer_call(x, l, n1, win, conv_w, sinks, rel_bias, poolw, pscale, wout, n2, w1b, w2b, bkt, fnorm)
    return x
```

```python
import functools
import math

import numpy as np
import jax
import jax.numpy as jnp
from jax import lax
from jax.experimental import pallas as pl
from jax.experimental.pallas import tpu as pltpu

D_MODEL = 1024
HEAD_DIM = 64
ATTN_WIDTH = 512
CONV_WIDTH = 256
CONV_K = 3
POOL_WIDTH = 256
N_Q_HEADS = 8
N_KV_HEADS = 2
GROUP = N_Q_HEADS // N_KV_HEADS
KV_WIDTH = N_KV_HEADS * HEAD_DIM
POOL_WINDOWS = (2, 4, 8, 16)
POOL_GROUP = 64
IN_WIDTH = 1792
D_FF = 4096
WINDOW = 128
BLOCK = 128
N_BUCKETS = 32
MAX_DISTANCE = 128
EPS = 1e-6
NEG = -1e30

_QKV_END = ATTN_WIDTH + 2 * KV_WIDTH
_GATES_WIDTH = IN_WIDTH - _QKV_END

TM = 512
KV_HALO = BLOCK
CONV_HALO = 8
POOL_HALO = 32
FF_CHUNK = 1024
V7X_VMEM_LIMIT_BYTES = 60 * 1024 * 1024
_PHASE_ORDER = "m m x m x m x m x m x x x m x m x m x m x m m x m m x m m m".split()

_F32 = jnp.float32
_BF16 = jnp.bfloat16


def _bucket_table():
    kj = np.arange(2 * BLOCK, dtype=np.int32)[:, None]
    qi = np.arange(BLOCK, dtype=np.int32)[None, :] + BLOCK
    dist = qi - kj
    n = np.maximum(dist, 0)
    max_exact = N_BUCKETS // 2
    nf = np.maximum(n, 1).astype(np.float32)
    large = max_exact + (np.log(nf / np.float32(max_exact)) / np.float32(math.log(MAX_DISTANCE / max_exact))
                         * np.float32(N_BUCKETS - max_exact)).astype(np.int32)
    large = np.minimum(large, N_BUCKETS - 1)
    bucket = np.where(n < max_exact, n, large)
    valid = (dist >= 0) & (dist < WINDOW)
    return np.where(valid, bucket, -1).astype(np.int32)


def _rmsnorm(x, g):
    ms = jnp.mean(x * x, axis=-1, keepdims=True)
    return x * lax.rsqrt(ms + EPS) * g


def _build_tables(relb_ref, poolw_ref, bkt_ref, bias_scr, wbd_scr):
    bkt = bkt_ref[...]
    for h in range(N_Q_HEADS):
        acc = jnp.where(bkt < 0, NEG, 0.0).astype(_F32)
        for b in range(N_BUCKETS):
            acc = jnp.where(bkt == b, relb_ref[b, h], acc)
        g, gi = divmod(h, GROUP)
        bias_scr[g, :, gi * BLOCK:(gi + 1) * BLOCK] = acc
    rows = lax.broadcasted_iota(jnp.int32, (POOL_GROUP, POOL_WIDTH), 0)
    cols = lax.broadcasted_iota(jnp.int32, (POOL_GROUP, POOL_WIDTH), 1)
    rep = jnp.where((cols % POOL_GROUP) == rows, 1.0, 0.0).astype(_BF16)
    tiled = jnp.dot(poolw_ref[...].astype(_BF16), rep, preferred_element_type=_F32)
    r2 = lax.broadcasted_iota(jnp.int32, (POOL_WIDTH, POOL_WIDTH), 0) // POOL_GROUP
    c2 = lax.broadcasted_iota(jnp.int32, (POOL_WIDTH, POOL_WIDTH), 1) // POOL_GROUP
    wbd_scr[...] = jnp.where(r2 == c2, tiled, 0.0).astype(_BF16)


def _attention_scores(n, g, first_tile, q_scr, k_scr, bias_scr, sink_row):
    r0 = n * BLOCK
    kband = k_scr[r0:r0 + 2 * BLOCK, :]
    zeros = jnp.zeros((HEAD_DIM, GROUP * BLOCK), _F32)
    scale = 1.0 / math.sqrt(HEAD_DIM)
    qg_t = (q_scr[r0:r0 + BLOCK, g * GROUP * HEAD_DIM:(g + 1) * GROUP * HEAD_DIM] * scale).T
    qcat = jnp.concatenate([qg_t[gi * HEAD_DIM:(gi + 1) * HEAD_DIM, :] for gi in range(GROUP)], axis=1)
    rhs = jnp.concatenate([qcat, zeros] if g == 0 else [zeros, qcat], axis=0).astype(_BF16)
    s_t = jnp.dot(kband, rhs, preferred_element_type=_F32) + bias_scr[g]
    if n == 0:
        s_t = jnp.concatenate([s_t[:BLOCK] + jnp.where(first_tile, NEG, 0.0), s_t[BLOCK:]], axis=0)
    m = jnp.maximum(jnp.max(s_t, axis=0, keepdims=True), sink_row)
    e = jnp.exp(s_t - m)
    denom = jnp.sum(e, axis=0, keepdims=True) + jnp.exp(sink_row - m)
    return e.astype(_BF16), 1.0 / denom


def _attention_output(n, g, e, inv_denom, v_scr, mixed_scr):
    r0 = n * BLOCK
    v_t = v_scr[r0:r0 + 2 * BLOCK, :].T[g * HEAD_DIM:(g + 1) * HEAD_DIM, :].astype(_BF16)
    o_t = jnp.dot(v_t, e, preferred_element_type=_F32) * inv_denom
    o4 = jnp.concatenate([o_t[:, gi * BLOCK:(gi + 1) * BLOCK] for gi in range(GROUP)], axis=0)
    mixed_scr[r0:r0 + BLOCK, g * GROUP * HEAD_DIM:(g + 1) * GROUP * HEAD_DIM] = o4.T.astype(_BF16)


def _mlp_half(x1_ref, h2_ref, w1_ref, w2_ref, fnorm_ref, o_ref):
    half = FF_CHUNK // 2
    halfd = D_MODEL // 2
    acc = [None, None]
    for c in range(D_FF // FF_CHUNK):
        acts = []
        for i in range(2):
            lo = c * FF_CHUNK + i * half
            hid = jnp.dot(h2_ref[...], w1_ref[:, lo:lo + half], preferred_element_type=_F32)
            acts.append(jnp.square(jnp.maximum(hid, 0.0)).astype(_BF16))
            yield
        act = jnp.concatenate(acts, axis=1)
        for i in range(2):
            part = jnp.dot(act, w2_ref[c * FF_CHUNK:(c + 1) * FF_CHUNK, i * halfd:(i + 1) * halfd],
                           preferred_element_type=_F32)
            acc[i] = part if acc[i] is None else acc[i] + part
            yield
    out = x1_ref[...] + jnp.concatenate(acc, axis=1)
    if fnorm_ref is not None:
        out = _rmsnorm(out, fnorm_ref[...])
    o_ref[0] = out
    yield


def _mixer_half(x_ref, x1_ref, h2_ref, seq_tile, n1_ref, win_ref, convw_ref, sinks_ref, pscale_ref, wout_ref,
                n2_ref, layer, q_scr, k_scr, v_scr, g_scr, u_scr, p_scr, a1_scr, a2_scr, a3_scr, mixed_scr,
                bias_scr, wbd_scr):
    first_tile = seq_tile == 0

    x = x_ref[0]
    h = _rmsnorm(x, n1_ref[...]).astype(_BF16)
    q_scr[...] = jnp.dot(h, win_ref[:, 0:ATTN_WIDTH], preferred_element_type=_F32)
    for r in range(0, TM, TM // 2):
        kv = jnp.dot(h[r:r + TM // 2], win_ref[:, ATTN_WIDTH:_QKV_END], preferred_element_type=_F32)
        k_scr[KV_HALO + r:KV_HALO + r + TM // 2, :] = kv[:, 0:KV_WIDTH].astype(_BF16)
        v_scr[KV_HALO + r:KV_HALO + r + TM // 2, :] = kv[:, KV_WIDTH:2 * KV_WIDTH]
    yield

    lane = lax.broadcasted_iota(jnp.int32, (1, GROUP * BLOCK), 1)
    sink_rows = []
    for g in range(N_KV_HEADS):
        row = jnp.full((1, GROUP * BLOCK), sinks_ref[layer, g * GROUP + GROUP - 1], _F32)
        for gi in range(GROUP - 2, -1, -1):
            row = jnp.where(lane < (gi + 1) * BLOCK, sinks_ref[layer, g * GROUP + gi], row)
        sink_rows.append(row)
    gates_half = _GATES_WIDTH // 2
    n_pairs = (TM // BLOCK) * N_KV_HEADS
    pending = None
    for k in range(n_pairs):
        n, g = divmod(k, N_KV_HEADS)
        if pending is not None:
            _attention_output(*pending, v_scr, mixed_scr)
        e, inv_denom = _attention_scores(n, g, first_tile, q_scr, k_scr, bias_scr, sink_rows[g])
        pending = (n, g, e, inv_denom)
        yield
        if k == n_pairs // 2 - 1:
            g_scr[:, 0:gates_half] = jnp.dot(h, win_ref[:, _QKV_END:_QKV_END + gates_half],
                                             preferred_element_type=_F32)
            yield
    _attention_output(*pending, v_scr, mixed_scr)
    g_scr[:, gates_half:_GATES_WIDTH] = jnp.dot(h, win_ref[:, _QKV_END + gates_half:IN_WIDTH],
                                                preferred_element_type=_F32)
    yield
    k_scr[0:KV_HALO, :] = k_scr[TM:TM + KV_HALO, :]
    v_scr[0:KV_HALO, :] = v_scr[TM:TM + KV_HALO, :]

    c0 = CONV_HALO
    u_scr[c0:c0 + TM, :] = g_scr[:, CONV_WIDTH:2 * CONV_WIDTH] * g_scr[:, 2 * CONV_WIDTH:3 * CONV_WIDTH]
    cw = convw_ref[...]
    y = (cw[0:1, :] * u_scr[c0 - 2:c0 - 2 + TM, :] + cw[1:2, :] * u_scr[c0 - 1:c0 - 1 + TM, :]
         + cw[2:3, :] * u_scr[c0:c0 + TM, :])
    mixed_scr[:, ATTN_WIDTH:ATTN_WIDTH + CONV_WIDTH] = (g_scr[:, 0:CONV_WIDTH] * y).astype(_BF16)
    u_scr[0:CONV_HALO, :] = u_scr[TM:TM + CONV_HALO, :]

    p0 = POOL_HALO
    pe = p0 + TM
    p = g_scr[:, 3 * CONV_WIDTH:_GATES_WIDTH]
    p_scr[p0:pe, :] = p
    a1_scr[8:pe, :] = p_scr[8:pe, :] + p_scr[7:pe - 1, :]
    a2_scr[16:pe, :] = a1_scr[16:pe, :] + a1_scr[14:pe - 2, :]
    a3_scr[24:pe, :] = a2_scr[24:pe, :] + a2_scr[20:pe - 4, :]
    s16 = a3_scr[p0:pe, :] + a3_scr[p0 - 8:pe - 8, :]
    plane = lax.broadcasted_iota(jnp.int32, (TM, POOL_WIDTH), 1)
    ssum = jnp.where(plane < POOL_GROUP, a1_scr[p0:pe, :],
                     jnp.where(plane < 2 * POOL_GROUP, a2_scr[p0:pe, :],
                               jnp.where(plane < 3 * POOL_GROUP, a3_scr[p0:pe, :], s16)))
    win = jnp.where(plane < POOL_GROUP, POOL_WINDOWS[0],
                    jnp.where(plane < 2 * POOL_GROUP, POOL_WINDOWS[1],
                              jnp.where(plane < 3 * POOL_GROUP, POOL_WINDOWS[2], POOL_WINDOWS[3])))
    t = seq_tile * TM + lax.broadcasted_iota(jnp.int32, (TM, POOL_WIDTH), 0)
    count = jnp.minimum(t + 1, win).astype(_F32)
    pooled = (ssum / count - p).astype(_BF16)
    for r in range(0, TM, TM // 2):
        mixed_pool = jnp.dot(pooled[r:r + TM // 2], wbd_scr[...], preferred_element_type=_F32) * pscale_ref[...]
        mixed_scr[r:r + TM // 2, ATTN_WIDTH + CONV_WIDTH:D_MODEL] = mixed_pool.astype(_BF16)
    p_scr[0:POOL_HALO, :] = p_scr[TM:TM + POOL_HALO, :]
    yield

    x1 = x + jnp.dot(mixed_scr[...], wout_ref[...], preferred_element_type=_F32)
    x1_ref[...] = x1
    h2_ref[...] = _rmsnorm(x1, n2_ref[...]).astype(_BF16)
    yield


def _layer_kernel(x_ref, n1_ref, win_ref, convw_ref, sinks_ref, relb_ref, poolw_ref, pscale_ref, wout_ref,
                  n2_ref, w1_ref, w2_ref, bkt_ref, *rest, layer, final, tiles_per_seq):
    if final:
        fnorm_ref, o_ref, *scratch = rest
    else:
        o_ref, *scratch = rest
    (x1_scr, h2_scr, q_scr, k_scr, v_scr, g_scr, u_scr, p_scr, a1_scr, a2_scr, a3_scr, mixed_scr, bias_scr,
     wbd_scr) = scratch
    s = pl.program_id(0)
    n_tiles = pl.num_programs(0) - 1
    slot = s % 2
    seq_tile = jnp.minimum(s, n_tiles - 1) % tiles_per_seq

    @pl.when(s == 0)
    def _():
        _build_tables(relb_ref, poolw_ref, bkt_ref, bias_scr, wbd_scr)

    @pl.when(jnp.logical_and(seq_tile == 0, s < n_tiles))
    def _():
        k_scr[0:KV_HALO, :] = jnp.zeros((KV_HALO, KV_WIDTH), _BF16)
        v_scr[0:KV_HALO, :] = jnp.zeros((KV_HALO, KV_WIDTH), _F32)
        u_scr[0:CONV_HALO, :] = jnp.zeros((CONV_HALO, CONV_WIDTH), _F32)
        p_scr[0:POOL_HALO, :] = jnp.zeros((POOL_HALO, POOL_WIDTH), _F32)

    def mlp_half():
        return _mlp_half(x1_scr.at[1 - slot], h2_scr.at[1 - slot], w1_ref, w2_ref, fnorm_ref if final else None,
                         o_ref)

    def mixer_half():
        return _mixer_half(x_ref, x1_scr.at[slot], h2_scr.at[slot], seq_tile, n1_ref, win_ref, convw_ref,
                           sinks_ref, pscale_ref, wout_ref, n2_ref, layer, q_scr, k_scr, v_scr, g_scr, u_scr, p_scr,
                           a1_scr, a2_scr, a3_scr, mixed_scr, bias_scr, wbd_scr)

    @pl.when(s == 0)
    def _():
        for _ in mixer_half():
            pass

    @pl.when(jnp.logical_and(s > 0, s < n_tiles))
    def _():
        mlp, mixer = mlp_half(), mixer_half()
        for who in _PHASE_ORDER:
            next(mlp if who == "m" else mixer)
        for half in (mlp, mixer):
            assert next(half, "done") == "done", "phase order does not cover every phase"

    @pl.when(s == n_tiles)
    def _():
        for _ in mlp_half():
            pass


def _resident(shape, layer=None):
    if layer is None:
        return pl.BlockSpec(shape, lambda s: (0,) * len(shape), pipeline_mode=pl.Buffered(1))
    return pl.BlockSpec((None,) + shape, lambda s: (layer,) + (0,) * len(shape), pipeline_mode=pl.Buffered(1))


def _layer_call(x, layer, n1, win, convw, sinks, relb, poolw, pscale, wout, n2, w1, w2, bkt, fnorm):
    batch, seq, _ = x.shape
    assert seq % TM == 0 and TM % BLOCK == 0
    tiles_per_seq = seq // TM
    n_tiles = batch * tiles_per_seq
    final = fnorm is not None
    smem = pl.BlockSpec(memory_space=pltpu.SMEM)

    def in_tile(s):
        t = jnp.minimum(s, n_tiles - 1)
        return (t // tiles_per_seq, t % tiles_per_seq, 0)

    def out_tile(s):
        t = jnp.maximum(s - 1, 0)
        return (t // tiles_per_seq, t % tiles_per_seq, 0)

    in_specs = [
        pl.BlockSpec((1, TM, D_MODEL), in_tile),
        _resident((1, D_MODEL), layer),
        _resident((D_MODEL, IN_WIDTH), layer),
        _resident((CONV_K, CONV_WIDTH), layer),
        smem,
        smem,
        _resident((POOL_WIDTH, POOL_GROUP), layer),
        _resident((1, POOL_WIDTH), layer),
        _resident((D_MODEL, D_MODEL), layer),
        _resident((1, D_MODEL), layer),
        _resident((D_MODEL, D_FF), layer),
        _resident((D_FF, D_MODEL), layer),
        _resident((2 * BLOCK, BLOCK)),
    ]
    args = [x, n1, win, convw, sinks, relb, poolw, pscale, wout, n2, w1, w2, bkt]
    if final:
        in_specs.append(_resident((1, D_MODEL)))
        args.append(fnorm)
    scratch = [
        pltpu.VMEM((2, TM, D_MODEL), _F32),
        pltpu.VMEM((2, TM, D_MODEL), _BF16),
        pltpu.VMEM((TM, ATTN_WIDTH), _F32),
        pltpu.VMEM((KV_HALO + TM, KV_WIDTH), _BF16),
        pltpu.VMEM((KV_HALO + TM, KV_WIDTH), _F32),
        pltpu.VMEM((TM, _GATES_WIDTH), _F32),
        pltpu.VMEM((CONV_HALO + TM, CONV_WIDTH), _F32),
        pltpu.VMEM((POOL_HALO + TM, POOL_WIDTH), _F32),
        pltpu.VMEM((POOL_HALO + TM, POOL_WIDTH), _F32),
        pltpu.VMEM((POOL_HALO + TM, POOL_WIDTH), _F32),
        pltpu.VMEM((POOL_HALO + TM, POOL_WIDTH), _F32),
        pltpu.VMEM((TM, D_MODEL), _BF16),
        pltpu.VMEM((N_KV_HEADS, 2 * BLOCK, GROUP * BLOCK), _F32),
        pltpu.VMEM((POOL_WIDTH, POOL_WIDTH), _BF16),
    ]
    return pl.pallas_call(
        functools.partial(_layer_kernel, layer=layer, final=final, tiles_per_seq=tiles_per_seq),
        grid=(n_tiles + 1,),
        in_specs=in_specs,
        out_specs=pl.BlockSpec((1, TM, D_MODEL), out_tile),
        out_shape=jax.ShapeDtypeStruct(x.shape, x.dtype),
        scratch_shapes=scratch,
        compiler_params=pltpu.CompilerParams(
            dimension_semantics=("arbitrary",),
            vmem_limit_bytes=V7X_VMEM_LIMIT_BYTES,
        ),
        name="hybrid_layer_final" if final else "hybrid_layer",
    )(*args)


def kernel(x, norm1, w_in, conv_w, sinks, pool_w, pool_scale, w_out, norm2, w1, w2, rel_bias, final_norm):
    depth = w_in.shape[0]
    bkt = jnp.asarray(_bucket_table())
    n1 = norm1.reshape(depth, 1, D_MODEL)
    n2 = norm2.reshape(depth, 1, D_MODEL)
    win = w_in.astype(_BF16)
    wout = w_out.astype(_BF16)
    w1b = w1.astype(_BF16)
    w2b = w2.astype(_BF16)
    poolw = pool_w.reshape(depth, POOL_WIDTH, POOL_GROUP)
    pscale = pool_scale.reshape(depth, 1, POOL_WIDTH)
    for l in range(depth):
        fnorm = final_norm.reshape(1, D_MODEL) if l == depth - 1 else None
        x = _layer_call(x, l, n1, win, conv_w, sinks, rel_bias, poolw, pscale, wout, n2, w1b, w2b, bkt, fnorm)
    return x
```

```python
import functools
import math

import numpy as np
import jax
import jax.numpy as jnp
from jax import lax
from jax.experimental import pallas as pl
from jax.experimental.pallas import tpu as pltpu

D_MODEL = 1024
HEAD_DIM = 64
ATTN_WIDTH = 512
CONV_WIDTH = 256
CONV_K = 3
POOL_WIDTH = 256
N_Q_HEADS = 8
N_KV_HEADS = 2
GROUP = N_Q_HEADS // N_KV_HEADS
KV_WIDTH = N_KV_HEADS * HEAD_DIM
POOL_WINDOWS = (2, 4, 8, 16)
POOL_GROUP = 64
IN_WIDTH = 1792
D_FF = 4096
WINDOW = 128
BLOCK = 128
N_BUCKETS = 32
MAX_DISTANCE = 128
EPS = 1e-6
NEG = -1e30

_QKV_END = ATTN_WIDTH + 2 * KV_WIDTH
_GATES_WIDTH = IN_WIDTH - _QKV_END

TM = 512
KV_HALO = BLOCK
CONV_HALO = 8
POOL_HALO = 32
FF_PIECE = 512
FF_CHUNK = 1024
V7X_VMEM_LIMIT_BYTES = 60 * 1024 * 1024
_PHASE_ORDER = "m x m x m x m x m x m x m x m x x x x x m m x m m x m m m m m".split()

_F32 = jnp.float32
_BF16 = jnp.bfloat16


def _bucket_table():
    kj = np.arange(2 * BLOCK, dtype=np.int32)[:, None]
    qi = np.arange(BLOCK, dtype=np.int32)[None, :] + BLOCK
    dist = qi - kj
    n = np.maximum(dist, 0)
    max_exact = N_BUCKETS // 2
    nf = np.maximum(n, 1).astype(np.float32)
    large = max_exact + (np.log(nf / np.float32(max_exact)) / np.float32(math.log(MAX_DISTANCE / max_exact))
                         * np.float32(N_BUCKETS - max_exact)).astype(np.int32)
    large = np.minimum(large, N_BUCKETS - 1)
    bucket = np.where(n < max_exact, n, large)
    valid = (dist >= 0) & (dist < WINDOW)
    return np.where(valid, bucket, -1).astype(np.int32)


def _rmsnorm(x, g):
    ms = jnp.mean(x * x, axis=-1, keepdims=True)
    return x * lax.rsqrt(ms + EPS) * g


def _build_tables(relb_ref, poolw_ref, bkt_ref, bias_scr, wbd_scr):
    bkt = bkt_ref[...]
    for h in range(N_Q_HEADS):
        acc = jnp.where(bkt < 0, NEG, 0.0).astype(_F32)
        for b in range(N_BUCKETS):
            acc = jnp.where(bkt == b, relb_ref[b, h], acc)
        g, gi = divmod(h, GROUP)
        bias_scr[g, :, gi * BLOCK:(gi + 1) * BLOCK] = acc
    rows = lax.broadcasted_iota(jnp.int32, (POOL_GROUP, POOL_WIDTH), 0)
    cols = lax.broadcasted_iota(jnp.int32, (POOL_GROUP, POOL_WIDTH), 1)
    rep = jnp.where((cols % POOL_GROUP) == rows, 1.0, 0.0).astype(_BF16)
    tiled = jnp.dot(poolw_ref[...].astype(_BF16), rep, preferred_element_type=_F32)
    r2 = lax.broadcasted_iota(jnp.int32, (POOL_WIDTH, POOL_WIDTH), 0) // POOL_GROUP
    c2 = lax.broadcasted_iota(jnp.int32, (POOL_WIDTH, POOL_WIDTH), 1) // POOL_GROUP
    wbd_scr[...] = jnp.where(r2 == c2, tiled, 0.0).astype(_BF16)


def _attention_scores(n, g, first_tile, q_scr, k_scr, bias_scr, sink_row):
    r0 = n * BLOCK
    kband = k_scr[r0:r0 + 2 * BLOCK, :]
    zeros = jnp.zeros((HEAD_DIM, GROUP * BLOCK), _F32)
    scale = 1.0 / math.sqrt(HEAD_DIM)
    qg_t = (q_scr[r0:r0 + BLOCK, g * GROUP * HEAD_DIM:(g + 1) * GROUP * HEAD_DIM] * scale).T
    qcat = jnp.concatenate([qg_t[gi * HEAD_DIM:(gi + 1) * HEAD_DIM, :] for gi in range(GROUP)], axis=1)
    rhs = jnp.concatenate([qcat, zeros] if g == 0 else [zeros, qcat], axis=0).astype(_BF16)
    s_t = jnp.dot(kband, rhs, preferred_element_type=_F32) + bias_scr[g]
    if n == 0:
        s_t = jnp.concatenate([s_t[:BLOCK] + jnp.where(first_tile, NEG, 0.0), s_t[BLOCK:]], axis=0)
    m = jnp.maximum(jnp.max(s_t, axis=0, keepdims=True), sink_row)
    e = jnp.exp(s_t - m)
    denom = jnp.sum(e, axis=0, keepdims=True) + jnp.exp(sink_row - m)
    return e.astype(_BF16), 1.0 / denom


def _attention_output(n, g, e, inv_denom, v_scr, mixed_scr):
    r0 = n * BLOCK
    v_t = v_scr[r0:r0 + 2 * BLOCK, :].T[g * HEAD_DIM:(g + 1) * HEAD_DIM, :].astype(_BF16)
    o_t = jnp.dot(v_t, e, preferred_element_type=_F32) * inv_denom
    o4 = jnp.concatenate([o_t[:, gi * BLOCK:(gi + 1) * BLOCK] for gi in range(GROUP)], axis=0)
    mixed_scr[r0:r0 + BLOCK, g * GROUP * HEAD_DIM:(g + 1) * GROUP * HEAD_DIM] = o4.T.astype(_BF16)


def _mlp_half(x1_ref, h2_ref, w1_ref, w2_ref, fnorm_ref, o_ref, act_scr):
    for lo in range(0, D_FF, FF_PIECE):
        hid = jnp.dot(h2_ref[...], w1_ref[:, lo:lo + FF_PIECE], preferred_element_type=_F32)
        act_scr[:, lo:lo + FF_PIECE] = jnp.square(jnp.maximum(hid, 0.0)).astype(_BF16)
        yield
    halfd = D_MODEL // 2
    acc = [None, None]
    for lo in range(0, D_FF, FF_CHUNK):
        for i in range(2):
            part = jnp.dot(act_scr[:, lo:lo + FF_CHUNK], w2_ref[lo:lo + FF_CHUNK, i * halfd:(i + 1) * halfd],
                           preferred_element_type=_F32)
            acc[i] = part if acc[i] is None else acc[i] + part
            yield
    out = x1_ref[...] + jnp.concatenate(acc, axis=1)
    if fnorm_ref is not None:
        out = _rmsnorm(out, fnorm_ref[...])
    o_ref[0] = out
    yield


def _mixer_half(x_ref, x1_ref, h2_ref, seq_tile, n1_ref, win_ref, convw_ref, sinks_ref, pscale_ref, wout_ref,
                n2_ref, layer, q_scr, k_scr, v_scr, g_scr, u_scr, p_scr, a1_scr, a2_scr, a3_scr, mixed_scr,
                bias_scr, wbd_scr):
    first_tile = seq_tile == 0

    x = x_ref[0]
    h = _rmsnorm(x, n1_ref[...]).astype(_BF16)
    q_scr[...] = jnp.dot(h, win_ref[:, 0:ATTN_WIDTH], preferred_element_type=_F32)
    for r in range(0, TM, TM // 2):
        kv = jnp.dot(h[r:r + TM // 2], win_ref[:, ATTN_WIDTH:_QKV_END], preferred_element_type=_F32)
        k_scr[KV_HALO + r:KV_HALO + r + TM // 2, :] = kv[:, 0:KV_WIDTH].astype(_BF16)
        v_scr[KV_HALO + r:KV_HALO + r + TM // 2, :] = kv[:, KV_WIDTH:2 * KV_WIDTH]
    yield

    lane = lax.broadcasted_iota(jnp.int32, (1, GROUP * BLOCK), 1)
    sink_rows = []
    for g in range(N_KV_HEADS):
        row = jnp.full((1, GROUP * BLOCK), sinks_ref[layer, g * GROUP + GROUP - 1], _F32)
        for gi in range(GROUP - 2, -1, -1):
            row = jnp.where(lane < (gi + 1) * BLOCK, sinks_ref[layer, g * GROUP + gi], row)
        sink_rows.append(row)
    gates_half = _GATES_WIDTH // 2
    n_pairs = (TM // BLOCK) * N_KV_HEADS
    pending = None
    for k in range(n_pairs):
        n, g = divmod(k, N_KV_HEADS)
        if pending is not None:
            _attention_output(*pending, v_scr, mixed_scr)
        e, inv_denom = _attention_scores(n, g, first_tile, q_scr, k_scr, bias_scr, sink_rows[g])
        pending = (n, g, e, inv_denom)
        yield
        i = k - (n_pairs - 2)
        if i >= 0:
            lo = _QKV_END + i * gates_half
            g_scr[:, i * gates_half:(i + 1) * gates_half] = jnp.dot(h, win_ref[:, lo:lo + gates_half],
                                                                    preferred_element_type=_F32)
            yield
    _attention_output(*pending, v_scr, mixed_scr)
    k_scr[0:KV_HALO, :] = k_scr[TM:TM + KV_HALO, :]
    v_scr[0:KV_HALO, :] = v_scr[TM:TM + KV_HALO, :]

    c0 = CONV_HALO
    u_scr[c0:c0 + TM, :] = g_scr[:, CONV_WIDTH:2 * CONV_WIDTH] * g_scr[:, 2 * CONV_WIDTH:3 * CONV_WIDTH]
    cw = convw_ref[...]
    y = (cw[0:1, :] * u_scr[c0 - 2:c0 - 2 + TM, :] + cw[1:2, :] * u_scr[c0 - 1:c0 - 1 + TM, :]
         + cw[2:3, :] * u_scr[c0:c0 + TM, :])
    mixed_scr[:, ATTN_WIDTH:ATTN_WIDTH + CONV_WIDTH] = (g_scr[:, 0:CONV_WIDTH] * y).astype(_BF16)
    u_scr[0:CONV_HALO, :] = u_scr[TM:TM + CONV_HALO, :]

    p0 = POOL_HALO
    pe = p0 + TM
    p = g_scr[:, 3 * CONV_WIDTH:_GATES_WIDTH]
    p_scr[p0:pe, :] = p
    a1_scr[8:pe, :] = p_scr[8:pe, :] + p_scr[7:pe - 1, :]
    a2_scr[16:pe, :] = a1_scr[16:pe, :] + a1_scr[14:pe - 2, :]
    a3_scr[24:pe, :] = a2_scr[24:pe, :] + a2_scr[20:pe - 4, :]
    s16 = a3_scr[p0:pe, :] + a3_scr[p0 - 8:pe - 8, :]
    plane = lax.broadcasted_iota(jnp.int32, (TM, POOL_WIDTH), 1)
    ssum = jnp.where(plane < POOL_GROUP, a1_scr[p0:pe, :],
                     jnp.where(plane < 2 * POOL_GROUP, a2_scr[p0:pe, :],
                               jnp.where(plane < 3 * POOL_GROUP, a3_scr[p0:pe, :], s16)))
    win = jnp.where(plane < POOL_GROUP, POOL_WINDOWS[0],
                    jnp.where(plane < 2 * POOL_GROUP, POOL_WINDOWS[1],
                              jnp.where(plane < 3 * POOL_GROUP, POOL_WINDOWS[2], POOL_WINDOWS[3])))
    t = seq_tile * TM + lax.broadcasted_iota(jnp.int32, (TM, POOL_WIDTH), 0)
    count = jnp.minimum(t + 1, win).astype(_F32)
    pooled = (ssum / count - p).astype(_BF16)
    p_scr[0:POOL_HALO, :] = p_scr[TM:TM + POOL_HALO, :]
    yield
    for r in range(0, TM, TM // 2):
        mixed_pool = jnp.dot(pooled[r:r + TM // 2], wbd_scr[...], preferred_element_type=_F32) * pscale_ref[...]
        mixed_scr[r:r + TM // 2, ATTN_WIDTH + CONV_WIDTH:D_MODEL] = mixed_pool.astype(_BF16)
    yield

    x1 = x + jnp.dot(mixed_scr[...], wout_ref[...], preferred_element_type=_F32)
    x1_ref[...] = x1
    h2_ref[...] = _rmsnorm(x1, n2_ref[...]).astype(_BF16)
    yield


def _layer_kernel(x_ref, n1_ref, win_ref, convw_ref, sinks_ref, relb_ref, poolw_ref, pscale_ref, wout_ref,
                  n2_ref, w1_ref, w2_ref, bkt_ref, *rest, layer, final, tiles_per_seq):
    if final:
        fnorm_ref, o_ref, *scratch = rest
    else:
        o_ref, *scratch = rest
    (x1_scr, h2_scr, act_scr, q_scr, k_scr, v_scr, g_scr, u_scr, p_scr, a1_scr, a2_scr, a3_scr, mixed_scr,
     bias_scr, wbd_scr) = scratch
    s = pl.program_id(0)
    n_tiles = pl.num_programs(0) - 1
    slot = s % 2
    seq_tile = jnp.minimum(s, n_tiles - 1) % tiles_per_seq

    @pl.when(s == 0)
    def _():
        _build_tables(relb_ref, poolw_ref, bkt_ref, bias_scr, wbd_scr)

    @pl.when(jnp.logical_and(seq_tile == 0, s < n_tiles))
    def _():
        k_scr[0:KV_HALO, :] = jnp.zeros((KV_HALO, KV_WIDTH), _BF16)
        v_scr[0:KV_HALO, :] = jnp.zeros((KV_HALO, KV_WIDTH), _F32)
        u_scr[0:CONV_HALO, :] = jnp.zeros((CONV_HALO, CONV_WIDTH), _F32)
        p_scr[0:POOL_HALO, :] = jnp.zeros((POOL_HALO, POOL_WIDTH), _F32)

    def mlp_half():
        return _mlp_half(x1_scr.at[1 - slot], h2_scr.at[1 - slot], w1_ref, w2_ref, fnorm_ref if final else None,
                         o_ref, act_scr)

    def mixer_half():
        return _mixer_half(x_ref, x1_scr.at[slot], h2_scr.at[slot], seq_tile, n1_ref, win_ref, convw_ref,
                           sinks_ref, pscale_ref, wout_ref, n2_ref, layer, q_scr, k_scr, v_scr, g_scr, u_scr, p_scr,
                           a1_scr, a2_scr, a3_scr, mixed_scr, bias_scr, wbd_scr)

    @pl.when(s == 0)
    def _():
        for _ in mixer_half():
            pass

    @pl.when(jnp.logical_and(s > 0, s < n_tiles))
    def _():
        mlp, mixer = mlp_half(), mixer_half()
        for who in _PHASE_ORDER:
            next(mlp if who == "m" else mixer)
        for half in (mlp, mixer):
            assert next(half, "done") == "done", "phase order does not cover every phase"

    @pl.when(s == n_tiles)
    def _():
        for _ in mlp_half():
            pass


def _resident(shape, layer=None):
    if layer is None:
        return pl.BlockSpec(shape, lambda s: (0,) * len(shape), pipeline_mode=pl.Buffered(1))
    return pl.BlockSpec((None,) + shape, lambda s: (layer,) + (0,) * len(shape), pipeline_mode=pl.Buffered(1))


def _layer_call(x, layer, n1, win, convw, sinks, relb, poolw, pscale, wout, n2, w1, w2, bkt, fnorm):
    batch, seq, _ = x.shape
    assert seq % TM == 0 and TM % BLOCK == 0
    tiles_per_seq = seq // TM
    n_tiles = batch * tiles_per_seq
    final = fnorm is not None
    smem = pl.BlockSpec(memory_space=pltpu.SMEM)

    def in_tile(s):
        t = jnp.minimum(s, n_tiles - 1)
        return (t // tiles_per_seq, t % tiles_per_seq, 0)

    def out_tile(s):
        t = jnp.maximum(s - 1, 0)
        return (t // tiles_per_seq, t % tiles_per_seq, 0)

    in_specs = [
        pl.BlockSpec((1, TM, D_MODEL), in_tile),
        _resident((1, D_MODEL), layer),
        _resident((D_MODEL, IN_WIDTH), layer),
        _resident((CONV_K, CONV_WIDTH), layer),
        smem,
        smem,
        _resident((POOL_WIDTH, POOL_GROUP), layer),
        _resident((1, POOL_WIDTH), layer),
        _resident((D_MODEL, D_MODEL), layer),
        _resident((1, D_MODEL), layer),
        _resident((D_MODEL, D_FF), layer),
        _resident((D_FF, D_MODEL), layer),
        _resident((2 * BLOCK, BLOCK)),
    ]
    args = [x, n1, win, convw, sinks, relb, poolw, pscale, wout, n2, w1, w2, bkt]
    if final:
        in_specs.append(_resident((1, D_MODEL)))
        args.append(fnorm)
    scratch = [
        pltpu.VMEM((2, TM, D_MODEL), _F32),
        pltpu.VMEM((2, TM, D_MODEL), _BF16),
        pltpu.VMEM((TM, D_FF), _BF16),
        pltpu.VMEM((TM, ATTN_WIDTH), _F32),
        pltpu.VMEM((KV_HALO + TM, KV_WIDTH), _BF16),
        pltpu.VMEM((KV_HALO + TM, KV_WIDTH), _F32),
        pltpu.VMEM((TM, _GATES_WIDTH), _F32),
        pltpu.VMEM((CONV_HALO + TM, CONV_WIDTH), _F32),
        pltpu.VMEM((POOL_HALO + TM, POOL_WIDTH), _F32),
        pltpu.VMEM((POOL_HALO + TM, POOL_WIDTH), _F32),
        pltpu.VMEM((POOL_HALO + TM, POOL_WIDTH), _F32),
        pltpu.VMEM((POOL_HALO + TM, POOL_WIDTH), _F32),
        pltpu.VMEM((TM, D_MODEL), _BF16),
        pltpu.VMEM((N_KV_HEADS, 2 * BLOCK, GROUP * BLOCK), _F32),
        pltpu.VMEM((POOL_WIDTH, POOL_WIDTH), _BF16),
    ]
    return pl.pallas_call(
        functools.partial(_layer_kernel, layer=layer, final=final, tiles_per_seq=tiles_per_seq),
        grid=(n_tiles + 1,),
        in_specs=in_specs,
        out_specs=pl.BlockSpec((1, TM, D_MODEL), out_tile),
        out_shape=jax.ShapeDtypeStruct(x.shape, x.dtype),
        scratch_shapes=scratch,
        compiler_params=pltpu.CompilerParams(
            dimension_semantics=("arbitrary",),
            vmem_limit_bytes=V7X_VMEM_LIMIT_BYTES,
        ),
        name="hybrid_layer_final" if final else "hybrid_layer",
    )(*args)


def kernel(x, norm1, w_in, conv_w, sinks, pool_w, pool_scale, w_out, norm2, w1, w2, rel_bias, final_norm):
    depth = w_in.shape[0]
    bkt = jnp.asarray(_bucket_table())
    n1 = norm1.reshape(depth, 1, D_MODEL)
    n2 = norm2.reshape(depth, 1, D_MODEL)
    win = w_in.astype(_BF16)
    wout = w_out.astype(_BF16)
    w1b = w1.astype(_BF16)
    w2b = w2.astype(_BF16)
    poolw = pool_w.reshape(depth, POOL_WIDTH, POOL_GROUP)
    pscale = pool_scale.reshape(depth, 1, POOL_WIDTH)
    for l in range(depth):
        fnorm = final_norm.reshape(1, D_MODEL) if l == depth - 1 else None
        x = _layer_call(x, l, n1, win, conv_w, sinks, rel_bias, poolw, pscale, wout, n2, w1b, w2b, bkt, fnorm)
    return x
```

```python
import functools
import math

import numpy as np
import jax
import jax.numpy as jnp
from jax import lax
from jax.experimental import pallas as pl
from jax.experimental.pallas import tpu as pltpu

D_MODEL = 1024
HEAD_DIM = 64
ATTN_WIDTH = 512
CONV_WIDTH = 256
CONV_K = 3
POOL_WIDTH = 256
N_Q_HEADS = 8
N_KV_HEADS = 2
GROUP = N_Q_HEADS // N_KV_HEADS
KV_WIDTH = N_KV_HEADS * HEAD_DIM
POOL_WINDOWS = (2, 4, 8, 16)
POOL_GROUP = 64
IN_WIDTH = 1792
D_FF = 4096
WINDOW = 128
BLOCK = 128
N_BUCKETS = 32
MAX_DISTANCE = 128
EPS = 1e-6
NEG = -1e30

_QKV_END = ATTN_WIDTH + 2 * KV_WIDTH
_GATES_WIDTH = IN_WIDTH - _QKV_END

TM = 512
KV_HALO = BLOCK
CONV_HALO = 8
POOL_HALO = 32
FF_PIECE = 512
FF_CHUNK = 1024
V7X_VMEM_LIMIT_BYTES = 60 * 1024 * 1024
_PHASE_ORDER = "m x m x m x m x m x m x m x m x x x x x m x x m m".split()

_F32 = jnp.float32
_BF16 = jnp.bfloat16


def _bucket_table():
    kj = np.arange(2 * BLOCK, dtype=np.int32)[:, None]
    qi = np.arange(BLOCK, dtype=np.int32)[None, :] + BLOCK
    dist = qi - kj
    n = np.maximum(dist, 0)
    max_exact = N_BUCKETS // 2
    nf = np.maximum(n, 1).astype(np.float32)
    large = max_exact + (np.log(nf / np.float32(max_exact)) / np.float32(math.log(MAX_DISTANCE / max_exact))
                         * np.float32(N_BUCKETS - max_exact)).astype(np.int32)
    large = np.minimum(large, N_BUCKETS - 1)
    bucket = np.where(n < max_exact, n, large)
    valid = (dist >= 0) & (dist < WINDOW)
    return np.where(valid, bucket, -1).astype(np.int32)


def _rmsnorm(x, g):
    ms = jnp.mean(x * x, axis=-1, keepdims=True)
    return x * lax.rsqrt(ms + EPS) * g


def _build_tables(relb_ref, poolw_ref, bkt_ref, bias_scr, wbd_scr):
    bkt = bkt_ref[...]
    for h in range(N_Q_HEADS):
        acc = jnp.where(bkt < 0, NEG, 0.0).astype(_F32)
        for b in range(N_BUCKETS):
            acc = jnp.where(bkt == b, relb_ref[b, h], acc)
        g, gi = divmod(h, GROUP)
        bias_scr[g, :, gi * BLOCK:(gi + 1) * BLOCK] = acc
    rows = lax.broadcasted_iota(jnp.int32, (POOL_GROUP, POOL_WIDTH), 0)
    cols = lax.broadcasted_iota(jnp.int32, (POOL_GROUP, POOL_WIDTH), 1)
    rep = jnp.where((cols % POOL_GROUP) == rows, 1.0, 0.0).astype(_BF16)
    tiled = jnp.dot(poolw_ref[...].astype(_BF16), rep, preferred_element_type=_F32)
    r2 = lax.broadcasted_iota(jnp.int32, (POOL_WIDTH, POOL_WIDTH), 0) // POOL_GROUP
    c2 = lax.broadcasted_iota(jnp.int32, (POOL_WIDTH, POOL_WIDTH), 1) // POOL_GROUP
    wbd_scr[...] = jnp.where(r2 == c2, tiled, 0.0).astype(_BF16)


def _attention_scores(n, g, first_tile, q_scr, k_scr, bias_scr, sink_row):
    r0 = n * BLOCK
    kband = k_scr[r0:r0 + 2 * BLOCK, :]
    zeros = jnp.zeros((HEAD_DIM, GROUP * BLOCK), _F32)
    scale = 1.0 / math.sqrt(HEAD_DIM)
    qg_t = (q_scr[r0:r0 + BLOCK, g * GROUP * HEAD_DIM:(g + 1) * GROUP * HEAD_DIM] * scale).T
    qcat = jnp.concatenate([qg_t[gi * HEAD_DIM:(gi + 1) * HEAD_DIM, :] for gi in range(GROUP)], axis=1)
    rhs = jnp.concatenate([qcat, zeros] if g == 0 else [zeros, qcat], axis=0).astype(_BF16)
    s_t = jnp.dot(kband, rhs, preferred_element_type=_F32) + bias_scr[g]
    if n == 0:
        s_t = jnp.concatenate([s_t[:BLOCK] + jnp.where(first_tile, NEG, 0.0), s_t[BLOCK:]], axis=0)
    m = jnp.maximum(jnp.max(s_t, axis=0, keepdims=True), sink_row)
    e = jnp.exp(s_t - m)
    denom = jnp.sum(e, axis=0, keepdims=True) + jnp.exp(sink_row - m)
    return e.astype(_BF16), 1.0 / denom


def _attention_output(n, g, e, inv_denom, v_scr, mixed_scr):
    r0 = n * BLOCK
    v_t = v_scr[r0:r0 + 2 * BLOCK, :].T[g * HEAD_DIM:(g + 1) * HEAD_DIM, :].astype(_BF16)
    o_t = jnp.dot(v_t, e, preferred_element_type=_F32) * inv_denom
    o4 = jnp.concatenate([o_t[:, gi * BLOCK:(gi + 1) * BLOCK] for gi in range(GROUP)], axis=0)
    mixed_scr[r0:r0 + BLOCK, g * GROUP * HEAD_DIM:(g + 1) * GROUP * HEAD_DIM] = o4.T.astype(_BF16)


def _mlp_half(x1_ref, h2_ref, w1_ref, w2_ref, fnorm_ref, o_ref, act_scr):
    for lo in range(0, D_FF, FF_PIECE):
        hid = jnp.dot(h2_ref[...], w1_ref[:, lo:lo + FF_PIECE], preferred_element_type=_F32)
        act_scr[:, lo:lo + FF_PIECE] = jnp.square(jnp.maximum(hid, 0.0)).astype(_BF16)
        yield
    halfd = D_MODEL // 2
    parts = []
    for i in range(2):
        parts.append(jnp.dot(act_scr[...], w2_ref[:, i * halfd:(i + 1) * halfd], preferred_element_type=_F32))
        yield
    out = x1_ref[...] + jnp.concatenate(parts, axis=1)
    if fnorm_ref is not None:
        out = _rmsnorm(out, fnorm_ref[...])
    o_ref[0] = out
    yield


def _mixer_half(x_ref, x1_ref, h2_ref, seq_tile, n1_ref, win_ref, convw_ref, sinks_ref, pscale_ref, wout_ref,
                n2_ref, layer, q_scr, k_scr, v_scr, g_scr, u_scr, p_scr, a1_scr, a2_scr, a3_scr, mixed_scr,
                bias_scr, wbd_scr):
    first_tile = seq_tile == 0

    x = x_ref[0]
    h = _rmsnorm(x, n1_ref[...]).astype(_BF16)
    q_scr[...] = jnp.dot(h, win_ref[:, 0:ATTN_WIDTH], preferred_element_type=_F32)
    for r in range(0, TM, TM // 2):
        kv = jnp.dot(h[r:r + TM // 2], win_ref[:, ATTN_WIDTH:_QKV_END], preferred_element_type=_F32)
        k_scr[KV_HALO + r:KV_HALO + r + TM // 2, :] = kv[:, 0:KV_WIDTH].astype(_BF16)
        v_scr[KV_HALO + r:KV_HALO + r + TM // 2, :] = kv[:, KV_WIDTH:2 * KV_WIDTH]
    yield

    lane = lax.broadcasted_iota(jnp.int32, (1, GROUP * BLOCK), 1)
    sink_rows = []
    for g in range(N_KV_HEADS):
        row = jnp.full((1, GROUP * BLOCK), sinks_ref[layer, g * GROUP + GROUP - 1], _F32)
        for gi in range(GROUP - 2, -1, -1):
            row = jnp.where(lane < (gi + 1) * BLOCK, sinks_ref[layer, g * GROUP + gi], row)
        sink_rows.append(row)
    gates_half = _GATES_WIDTH // 2
    n_pairs = (TM // BLOCK) * N_KV_HEADS
    pending = None
    for k in range(n_pairs):
        n, g = divmod(k, N_KV_HEADS)
        if pending is not None:
            _attention_output(*pending, v_scr, mixed_scr)
        e, inv_denom = _attention_scores(n, g, first_tile, q_scr, k_scr, bias_scr, sink_rows[g])
        pending = (n, g, e, inv_denom)
        yield
        i = k - (n_pairs - 2)
        if i >= 0:
            lo = _QKV_END + i * gates_half
            g_scr[:, i * gates_half:(i + 1) * gates_half] = jnp.dot(h, win_ref[:, lo:lo + gates_half],
                                                                    preferred_element_type=_F32)
            yield
    _attention_output(*pending, v_scr, mixed_scr)
    k_scr[0:KV_HALO, :] = k_scr[TM:TM + KV_HALO, :]
    v_scr[0:KV_HALO, :] = v_scr[TM:TM + KV_HALO, :]

    c0 = CONV_HALO
    u_scr[c0:c0 + TM, :] = g_scr[:, CONV_WIDTH:2 * CONV_WIDTH] * g_scr[:, 2 * CONV_WIDTH:3 * CONV_WIDTH]
    cw = convw_ref[...]
    y = (cw[0:1, :] * u_scr[c0 - 2:c0 - 2 + TM, :] + cw[1:2, :] * u_scr[c0 - 1:c0 - 1 + TM, :]
         + cw[2:3, :] * u_scr[c0:c0 + TM, :])
    mixed_scr[:, ATTN_WIDTH:ATTN_WIDTH + CONV_WIDTH] = (g_scr[:, 0:CONV_WIDTH] * y).astype(_BF16)
    u_scr[0:CONV_HALO, :] = u_scr[TM:TM + CONV_HALO, :]

    p0 = POOL_HALO
    pe = p0 + TM
    p = g_scr[:, 3 * CONV_WIDTH:_GATES_WIDTH]
    p_scr[p0:pe, :] = p
    a1_scr[8:pe, :] = p_scr[8:pe, :] + p_scr[7:pe - 1, :]
    a2_scr[16:pe, :] = a1_scr[16:pe, :] + a1_scr[14:pe - 2, :]
    a3_scr[24:pe, :] = a2_scr[24:pe, :] + a2_scr[20:pe - 4, :]
    s16 = a3_scr[p0:pe, :] + a3_scr[p0 - 8:pe - 8, :]
    plane = lax.broadcasted_iota(jnp.int32, (TM, POOL_WIDTH), 1)
    ssum = jnp.where(plane < POOL_GROUP, a1_scr[p0:pe, :],
                     jnp.where(plane < 2 * POOL_GROUP, a2_scr[p0:pe, :],
                               jnp.where(plane < 3 * POOL_GROUP, a3_scr[p0:pe, :], s16)))
    win = jnp.where(plane < POOL_GROUP, POOL_WINDOWS[0],
                    jnp.where(plane < 2 * POOL_GROUP, POOL_WINDOWS[1],
                              jnp.where(plane < 3 * POOL_GROUP, POOL_WINDOWS[2], POOL_WINDOWS[3])))
    t = seq_tile * TM + lax.broadcasted_iota(jnp.int32, (TM, POOL_WIDTH), 0)
    count = jnp.minimum(t + 1, win).astype(_F32)
    pooled = (ssum / count - p).astype(_BF16)
    p_scr[0:POOL_HALO, :] = p_scr[TM:TM + POOL_HALO, :]
    yield
    for r in range(0, TM, TM // 2):
        mixed_pool = jnp.dot(pooled[r:r + TM // 2], wbd_scr[...], preferred_element_type=_F32) * pscale_ref[...]
        mixed_scr[r:r + TM // 2, ATTN_WIDTH + CONV_WIDTH:D_MODEL] = mixed_pool.astype(_BF16)
    yield

    x1 = x + jnp.dot(mixed_scr[...], wout_ref[...], preferred_element_type=_F32)
    x1_ref[...] = x1
    h2_ref[...] = _rmsnorm(x1, n2_ref[...]).astype(_BF16)
    yield


def _layer_kernel(x_ref, n1_ref, win_ref, convw_ref, sinks_ref, relb_ref, poolw_ref, pscale_ref, wout_ref,
                  n2_ref, w1_ref, w2_ref, bkt_ref, *rest, layer, final, tiles_per_seq):
    if final:
        fnorm_ref, o_ref, *scratch = rest
    else:
        o_ref, *scratch = rest
    (x1_scr, h2_scr, act_scr, q_scr, k_scr, v_scr, g_scr, u_scr, p_scr, a1_scr, a2_scr, a3_scr, mixed_scr,
     bias_scr, wbd_scr) = scratch
    s = pl.program_id(0)
    n_tiles = pl.num_programs(0) - 1
    slot = s % 2
    seq_tile = jnp.minimum(s, n_tiles - 1) % tiles_per_seq

    @pl.when(s == 0)
    def _():
        _build_tables(relb_ref, poolw_ref, bkt_ref, bias_scr, wbd_scr)

    @pl.when(jnp.logical_and(seq_tile == 0, s < n_tiles))
    def _():
        k_scr[0:KV_HALO, :] = jnp.zeros((KV_HALO, KV_WIDTH), _BF16)
        v_scr[0:KV_HALO, :] = jnp.zeros((KV_HALO, KV_WIDTH), _F32)
        u_scr[0:CONV_HALO, :] = jnp.zeros((CONV_HALO, CONV_WIDTH), _F32)
        p_scr[0:POOL_HALO, :] = jnp.zeros((POOL_HALO, POOL_WIDTH), _F32)

    def mlp_half():
        return _mlp_half(x1_scr.at[1 - slot], h2_scr.at[1 - slot], w1_ref, w2_ref, fnorm_ref if final else None,
                         o_ref, act_scr)

    def mixer_half():
        return _mixer_half(x_ref, x1_scr.at[slot], h2_scr.at[slot], seq_tile, n1_ref, win_ref, convw_ref,
                           sinks_ref, pscale_ref, wout_ref, n2_ref, layer, q_scr, k_scr, v_scr, g_scr, u_scr, p_scr,
                           a1_scr, a2_scr, a3_scr, mixed_scr, bias_scr, wbd_scr)

    @pl.when(s == 0)
    def _():
        for _ in mixer_half():
            pass

    @pl.when(jnp.logical_and(s > 0, s < n_tiles))
    def _():
        mlp, mixer = mlp_half(), mixer_half()
        for who in _PHASE_ORDER:
            next(mlp if who == "m" else mixer)
        for half in (mlp, mixer):
            assert next(half, "done") == "done", "phase order does not cover every phase"

    @pl.when(s == n_tiles)
    def _():
        for _ in mlp_half():
            pass


def _resident(shape, layer=None):
    if layer is None:
        return pl.BlockSpec(shape, lambda s: (0,) * len(shape), pipeline_mode=pl.Buffered(1))
    return pl.BlockSpec((None,) + shape, lambda s: (layer,) + (0,) * len(shape), pipeline_mode=pl.Buffered(1))


def _layer_call(x, layer, n1, win, convw, sinks, relb, poolw, pscale, wout, n2, w1, w2, bkt, fnorm):
    batch, seq, _ = x.shape
    assert seq % TM == 0 and TM % BLOCK == 0
    tiles_per_seq = seq // TM
    n_tiles = batch * tiles_per_seq
    final = fnorm is not None
    smem = pl.BlockSpec(memory_space=pltpu.SMEM)

    def in_tile(s):
        t = jnp.minimum(s, n_tiles - 1)
        return (t // tiles_per_seq, t % tiles_per_seq, 0)

    def out_tile(s):
        t = jnp.maximum(s - 1, 0)
        return (t // tiles_per_seq, t % tiles_per_seq, 0)

    in_specs = [
        pl.BlockSpec((1, TM, D_MODEL), in_tile),
        _resident((1, D_MODEL), layer),
        _resident((D_MODEL, IN_WIDTH), layer),
        _resident((CONV_K, CONV_WIDTH), layer),
        smem,
        smem,
        _resident((POOL_WIDTH, POOL_GROUP), layer),
        _resident((1, POOL_WIDTH), layer),
        _resident((D_MODEL, D_MODEL), layer),
        _resident((1, D_MODEL), layer),
        _resident((D_MODEL, D_FF), layer),
        _resident((D_FF, D_MODEL), layer),
        _resident((2 * BLOCK, BLOCK)),
    ]
    args = [x, n1, win, convw, sinks, relb, poolw, pscale, wout, n2, w1, w2, bkt]
    if final:
        in_specs.append(_resident((1, D_MODEL)))
        args.append(fnorm)
    scratch = [
        pltpu.VMEM((2, TM, D_MODEL), _F32),
        pltpu.VMEM((2, TM, D_MODEL), _BF16),
        pltpu.VMEM((TM, D_FF), _BF16),
        pltpu.VMEM((TM, ATTN_WIDTH), _F32),
        pltpu.VMEM((KV_HALO + TM, KV_WIDTH), _BF16),
        pltpu.VMEM((KV_HALO + TM, KV_WIDTH), _F32),
        pltpu.VMEM((TM, _GATES_WIDTH), _F32),
        pltpu.VMEM((CONV_HALO + TM, CONV_WIDTH), _F32),
        pltpu.VMEM((POOL_HALO + TM, POOL_WIDTH), _F32),
        pltpu.VMEM((POOL_HALO + TM, POOL_WIDTH), _F32),
        pltpu.VMEM((POOL_HALO + TM, POOL_WIDTH), _F32),
        pltpu.VMEM((POOL_HALO + TM, POOL_WIDTH), _F32),
        pltpu.VMEM((TM, D_MODEL), _BF16),
        pltpu.VMEM((N_KV_HEADS, 2 * BLOCK, GROUP * BLOCK), _F32),
        pltpu.VMEM((POOL_WIDTH, POOL_WIDTH), _BF16),
    ]
    return pl.pallas_call(
        functools.partial(_layer_kernel, layer=layer, final=final, tiles_per_seq=tiles_per_seq),
        grid=(n_tiles + 1,),
        in_specs=in_specs,
        out_specs=pl.BlockSpec((1, TM, D_MODEL), out_tile),
        out_shape=jax.ShapeDtypeStruct(x.shape, x.dtype),
        scratch_shapes=scratch,
        compiler_params=pltpu.CompilerParams(
            dimension_semantics=("arbitrary",),
            vmem_limit_bytes=V7X_VMEM_LIMIT_BYTES,
        ),
        name="hybrid_layer_final" if final else "hybrid_layer",
    )(*args)


def kernel(x, norm1, w_in, conv_w, sinks, pool_w, pool_scale, w_out, norm2, w1, w2, rel_bias, final_norm):
    depth = w_in.shape[0]
    bkt = jnp.asarray(_bucket_table())
    n1 = norm1.reshape(depth, 1, D_MODEL)
    n2 = norm2.reshape(depth, 1, D_MODEL)
    win = w_in.astype(_BF16)
    wout = w_out.astype(_BF16)
    w1b = w1.astype(_BF16)
    w2b = w2.astype(_BF16)
    poolw = pool_w.reshape(depth, POOL_WIDTH, POOL_GROUP)
    pscale = pool_scale.reshape(depth, 1, POOL_WIDTH)
    for l in range(depth):
        fnorm = final_norm.reshape(1, D_MODEL) if l == depth - 1 else None
        x = _layer_call(x, l, n1, win, conv_w, sinks, rel_bias, poolw, pscale, wout, n2, w1b, w2b, bkt, fnorm)
    return x
```

```python
import functools
import math

import numpy as np
import jax
import jax.numpy as jnp
from jax import lax
from jax.experimental import pallas as pl
from jax.experimental.pallas import tpu as pltpu

D_MODEL = 1024
HEAD_DIM = 64
ATTN_WIDTH = 512
CONV_WIDTH = 256
CONV_K = 3
POOL_WIDTH = 256
N_Q_HEADS = 8
N_KV_HEADS = 2
GROUP = N_Q_HEADS // N_KV_HEADS
KV_WIDTH = N_KV_HEADS * HEAD_DIM
POOL_WINDOWS = (2, 4, 8, 16)
POOL_GROUP = 64
IN_WIDTH = 1792
D_FF = 4096
WINDOW = 128
BLOCK = 128
N_BUCKETS = 32
MAX_DISTANCE = 128
EPS = 1e-6
NEG = -1e30

_QKV_END = ATTN_WIDTH + 2 * KV_WIDTH
_GATES_WIDTH = IN_WIDTH - _QKV_END

TM = 512
KV_HALO = BLOCK
CONV_HALO = 8
POOL_HALO = 32
FF_PIECE = 512
BF16_SUBLANES = 16
V7X_VMEM_LIMIT_BYTES = 60 * 1024 * 1024
_PHASE_ORDER = "m x m x m x m x m x m x m x m x x x x x m x x m m".split()

_F32 = jnp.float32
_BF16 = jnp.bfloat16


def _bucket_table():
    kj = np.arange(2 * BLOCK, dtype=np.int32)[:, None]
    qi = np.arange(BLOCK, dtype=np.int32)[None, :] + BLOCK
    dist = qi - kj
    n = np.maximum(dist, 0)
    max_exact = N_BUCKETS // 2
    nf = np.maximum(n, 1).astype(np.float32)
    large = max_exact + (np.log(nf / np.float32(max_exact)) / np.float32(math.log(MAX_DISTANCE / max_exact))
                         * np.float32(N_BUCKETS - max_exact)).astype(np.int32)
    large = np.minimum(large, N_BUCKETS - 1)
    bucket = np.where(n < max_exact, n, large)
    valid = (dist >= 0) & (dist < WINDOW)
    return np.where(valid, bucket, -1).astype(np.int32)


def _rmsnorm(x, g):
    ms = jnp.mean(x * x, axis=-1, keepdims=True)
    return x * lax.rsqrt(ms + EPS) * g


def _build_tables(relb_ref, poolw_ref, bkt_ref, bias_scr, wbd_scr):
    bkt = bkt_ref[...]
    for h in range(N_Q_HEADS):
        acc = jnp.where(bkt < 0, NEG, 0.0).astype(_F32)
        for b in range(N_BUCKETS):
            acc = jnp.where(bkt == b, relb_ref[b, h], acc)
        g, gi = divmod(h, GROUP)
        bias_scr[g, :, gi * BLOCK:(gi + 1) * BLOCK] = acc
    rows = lax.broadcasted_iota(jnp.int32, (POOL_GROUP, POOL_WIDTH), 0)
    cols = lax.broadcasted_iota(jnp.int32, (POOL_GROUP, POOL_WIDTH), 1)
    rep = jnp.where((cols % POOL_GROUP) == rows, 1.0, 0.0).astype(_BF16)
    tiled = jnp.dot(poolw_ref[...].astype(_BF16), rep, preferred_element_type=_F32)
    r2 = lax.broadcasted_iota(jnp.int32, (POOL_WIDTH, POOL_WIDTH), 0) // POOL_GROUP
    c2 = lax.broadcasted_iota(jnp.int32, (POOL_WIDTH, POOL_WIDTH), 1) // POOL_GROUP
    wbd_scr[...] = jnp.where(r2 == c2, tiled, 0.0).astype(_BF16)


def _attention_scores(n, g, first_tile, q_scr, k_scr, bias_scr, sink_row):
    r0 = n * BLOCK
    kband = k_scr[r0:r0 + 2 * BLOCK, :]
    zeros = jnp.zeros((HEAD_DIM, GROUP * BLOCK), _F32)
    scale = 1.0 / math.sqrt(HEAD_DIM)
    qg_t = (q_scr[r0:r0 + BLOCK, g * GROUP * HEAD_DIM:(g + 1) * GROUP * HEAD_DIM] * scale).T
    qcat = jnp.concatenate([qg_t[gi * HEAD_DIM:(gi + 1) * HEAD_DIM, :] for gi in range(GROUP)], axis=1)
    rhs = jnp.concatenate([qcat, zeros] if g == 0 else [zeros, qcat], axis=0).astype(_BF16)
    s_t = jnp.dot(kband, rhs, preferred_element_type=_F32) + bias_scr[g]
    if n == 0:
        s_t = jnp.concatenate([s_t[:BLOCK] + jnp.where(first_tile, NEG, 0.0), s_t[BLOCK:]], axis=0)
    m = jnp.maximum(jnp.max(s_t, axis=0, keepdims=True), sink_row)
    e = jnp.exp(s_t - m)
    denom = jnp.sum(e, axis=0, keepdims=True) + jnp.exp(sink_row - m)
    return e.astype(_BF16), 1.0 / denom


def _attention_output(n, g, e, inv_denom, v_scr, mixed_scr):
    r0 = n * BLOCK
    v_t = v_scr[r0:r0 + 2 * BLOCK, :].T[g * HEAD_DIM:(g + 1) * HEAD_DIM, :].astype(_BF16)
    o_t = jnp.dot(v_t, e, preferred_element_type=_F32) * inv_denom
    o4 = jnp.concatenate([o_t[:, gi * BLOCK:(gi + 1) * BLOCK] for gi in range(GROUP)], axis=0)
    mixed_scr[r0:r0 + BLOCK, g * GROUP * HEAD_DIM:(g + 1) * GROUP * HEAD_DIM] = o4.T.astype(_BF16)


def _mlp_half(x1_ref, h2_ref, w1_ref, w2_ref, fnorm_ref, o_ref, act_scr):
    for lo in range(0, D_FF, FF_PIECE):
        hid = jnp.dot(h2_ref[...], w1_ref[:, lo:lo + FF_PIECE], preferred_element_type=_F32)
        act_scr[:, lo:lo + FF_PIECE] = jnp.square(jnp.maximum(hid, 0.0)).astype(_BF16)
        yield
    halfd = D_MODEL // 2
    parts = []
    for i in range(2):
        parts.append(jnp.dot(act_scr[...], w2_ref[:, i * halfd:(i + 1) * halfd], preferred_element_type=_F32))
        yield
    out = x1_ref[...] + jnp.concatenate(parts, axis=1)
    if fnorm_ref is not None:
        out = _rmsnorm(out, fnorm_ref[...])
    o_ref[0] = out
    yield


def _mixer_half(x_ref, x1_ref, h2_ref, seq_tile, n1_ref, win_ref, convw_ref, sinks_ref, pscale_ref, wout_ref,
                n2_ref, layer, q_scr, k_scr, v_scr, g_scr, u_scr, p_scr, a1_scr, a2_scr, a3_scr, mixed_scr,
                bias_scr, wbd_scr):
    first_tile = seq_tile == 0

    x = x_ref[0]
    h = _rmsnorm(x, n1_ref[...]).astype(_BF16)
    q_scr[...] = jnp.dot(h, win_ref[:, 0:ATTN_WIDTH], preferred_element_type=_F32)
    for r in range(0, TM, TM // 2):
        kv = jnp.dot(h[r:r + TM // 2], win_ref[:, ATTN_WIDTH:_QKV_END], preferred_element_type=_F32)
        k_scr[KV_HALO + r:KV_HALO + r + TM // 2, :] = kv[:, 0:KV_WIDTH].astype(_BF16)
        v_scr[KV_HALO + r:KV_HALO + r + TM // 2, :] = kv[:, KV_WIDTH:2 * KV_WIDTH]
    yield

    lane = lax.broadcasted_iota(jnp.int32, (1, GROUP * BLOCK), 1)
    sink_rows = []
    for g in range(N_KV_HEADS):
        row = jnp.full((1, GROUP * BLOCK), sinks_ref[layer, g * GROUP + GROUP - 1], _F32)
        for gi in range(GROUP - 2, -1, -1):
            row = jnp.where(lane < (gi + 1) * BLOCK, sinks_ref[layer, g * GROUP + gi], row)
        sink_rows.append(row)
    gates_half = _GATES_WIDTH // 2
    n_pairs = (TM // BLOCK) * N_KV_HEADS
    pending = None
    for k in range(n_pairs):
        n, g = divmod(k, N_KV_HEADS)
        if pending is not None:
            _attention_output(*pending, v_scr, mixed_scr)
        e, inv_denom = _attention_scores(n, g, first_tile, q_scr, k_scr, bias_scr, sink_rows[g])
        pending = (n, g, e, inv_denom)
        yield
        i = k - (n_pairs - 2)
        if i >= 0:
            lo = _QKV_END + i * gates_half
            g_scr[:, i * gates_half:(i + 1) * gates_half] = jnp.dot(h, win_ref[:, lo:lo + gates_half],
                                                                    preferred_element_type=_F32)
            yield
    _attention_output(*pending, v_scr, mixed_scr)
    k_scr[0:KV_HALO, :] = k_scr[TM:TM + KV_HALO, :]
    v_scr[0:KV_HALO, :] = v_scr[TM:TM + KV_HALO, :]

    c0 = CONV_HALO
    u_scr[c0:c0 + TM, :] = g_scr[:, CONV_WIDTH:2 * CONV_WIDTH] * g_scr[:, 2 * CONV_WIDTH:3 * CONV_WIDTH]
    cw = convw_ref[...]
    y = (cw[0:1, :] * u_scr[c0 - 2:c0 - 2 + TM, :] + cw[1:2, :] * u_scr[c0 - 1:c0 - 1 + TM, :]
         + cw[2:3, :] * u_scr[c0:c0 + TM, :])
    mixed_scr[:, ATTN_WIDTH:ATTN_WIDTH + CONV_WIDTH] = (g_scr[:, 0:CONV_WIDTH] * y).astype(_BF16)
    u_scr[0:CONV_HALO, :] = u_scr[TM:TM + CONV_HALO, :]

    p0 = POOL_HALO
    pe = p0 + TM
    p = g_scr[:, 3 * CONV_WIDTH:_GATES_WIDTH]
    p_scr[p0:pe, :] = p
    a1_scr[8:pe, :] = p_scr[8:pe, :] + p_scr[7:pe - 1, :]
    a2_scr[16:pe, :] = a1_scr[16:pe, :] + a1_scr[14:pe - 2, :]
    a3_scr[24:pe, :] = a2_scr[24:pe, :] + a2_scr[20:pe - 4, :]
    s16 = a3_scr[p0:pe, :] + a3_scr[p0 - 8:pe - 8, :]
    plane = lax.broadcasted_iota(jnp.int32, (TM, POOL_WIDTH), 1)
    ssum = jnp.where(plane < POOL_GROUP, a1_scr[p0:pe, :],
                     jnp.where(plane < 2 * POOL_GROUP, a2_scr[p0:pe, :],
                               jnp.where(plane < 3 * POOL_GROUP, a3_scr[p0:pe, :], s16)))
    win = jnp.where(plane < POOL_GROUP, POOL_WINDOWS[0],
                    jnp.where(plane < 2 * POOL_GROUP, POOL_WINDOWS[1],
                              jnp.where(plane < 3 * POOL_GROUP, POOL_WINDOWS[2], POOL_WINDOWS[3])))
    t = seq_tile * TM + lax.broadcasted_iota(jnp.int32, (TM, POOL_WIDTH), 0)
    count = jnp.minimum(t + 1, win).astype(_F32)
    pooled = (ssum / count - p).astype(_BF16)
    p_scr[0:POOL_HALO, :] = p_scr[TM:TM + POOL_HALO, :]
    yield
    for r in range(0, TM, TM // 2):
        mixed_pool = jnp.dot(pooled[r:r + TM // 2], wbd_scr[...], preferred_element_type=_F32) * pscale_ref[...]
        mixed_scr[r:r + TM // 2, ATTN_WIDTH + CONV_WIDTH:D_MODEL] = mixed_pool.astype(_BF16)
    yield

    x1 = x + jnp.dot(mixed_scr[...], wout_ref[...], preferred_element_type=_F32)
    x1_ref[...] = x1
    h2_ref[...] = _rmsnorm(x1, n2_ref[...]).astype(_BF16)
    yield


def _step_rows(s, buf):
    rows = buf.shape[0]
    return pl.ds(pl.multiple_of(s * rows, rows), rows)


def _fetch_copies(s, layer, srcs, in_bufs, sems):
    return [pltpu.make_async_copy(src.at[layer + 1, _step_rows(s, buf), :], buf, sems.at[i])
            for i, (src, buf) in enumerate(zip(srcs, in_bufs))]


def _put_copies(s, dsts, out_bufs, sems):
    return [pltpu.make_async_copy(buf, dst.at[_step_rows(s, buf), :], sems.at[i])
            for i, (dst, buf) in enumerate(zip(dsts, out_bufs))]


def _layer_kernel(x_ref, n1_ref, win_ref, convw_ref, sinks_ref, relb_ref, poolw_ref, pscale_ref, wout_ref,
                  n2_ref, w1_ref, w2_ref, bkt_ref, *rest, layer, final, cast_next, tiles_per_seq):
    rest = list(rest)
    fnorm_ref = rest.pop(0) if final else None
    next_f32 = [rest.pop(0) for _ in range(4)] if cast_next else []
    o_ref = rest.pop(0)
    next_bf16 = [rest.pop(0) for _ in range(4)] if cast_next else []
    (x1_scr, h2_scr, act_scr, q_scr, k_scr, v_scr, g_scr, u_scr, p_scr, a1_scr, a2_scr, a3_scr, mixed_scr,
     bias_scr, wbd_scr) = rest[:15]
    s = pl.program_id(0)
    n_tiles = pl.num_programs(0) - 1
    slot = s % 2
    seq_tile = jnp.minimum(s, n_tiles - 1) % tiles_per_seq

    if cast_next:
        in_bufs, out_bufs, (in_sems, out_sems) = rest[15:19], rest[19:23], rest[23:25]

        @pl.when(s < n_tiles)
        def _():
            for fetch in _fetch_copies(s, layer, next_f32, in_bufs, in_sems):
                fetch.start()

    @pl.when(s == 0)
    def _():
        _build_tables(relb_ref, poolw_ref, bkt_ref, bias_scr, wbd_scr)

    @pl.when(jnp.logical_and(seq_tile == 0, s < n_tiles))
    def _():
        k_scr[0:KV_HALO, :] = jnp.zeros((KV_HALO, KV_WIDTH), _BF16)
        v_scr[0:KV_HALO, :] = jnp.zeros((KV_HALO, KV_WIDTH), _F32)
        u_scr[0:CONV_HALO, :] = jnp.zeros((CONV_HALO, CONV_WIDTH), _F32)
        p_scr[0:POOL_HALO, :] = jnp.zeros((POOL_HALO, POOL_WIDTH), _F32)

    def mlp_half():
        return _mlp_half(x1_scr.at[1 - slot], h2_scr.at[1 - slot], w1_ref, w2_ref, fnorm_ref if final else None,
                         o_ref, act_scr)

    def mixer_half():
        return _mixer_half(x_ref, x1_scr.at[slot], h2_scr.at[slot], seq_tile, n1_ref, win_ref, convw_ref,
                           sinks_ref, pscale_ref, wout_ref, n2_ref, layer, q_scr, k_scr, v_scr, g_scr, u_scr, p_scr,
                           a1_scr, a2_scr, a3_scr, mixed_scr, bias_scr, wbd_scr)

    @pl.when(s == 0)
    def _():
        for _ in mixer_half():
            pass

    @pl.when(jnp.logical_and(s > 0, s < n_tiles))
    def _():
        mlp, mixer = mlp_half(), mixer_half()
        for who in _PHASE_ORDER:
            next(mlp if who == "m" else mixer)
        for half in (mlp, mixer):
            assert next(half, "done") == "done", "phase order does not cover every phase"

    @pl.when(s == n_tiles)
    def _():
        for _ in mlp_half():
            pass

    if cast_next:
        @pl.when(s < n_tiles)
        def _():
            @pl.when(s > 0)
            def _():
                for put in _put_copies(s - 1, next_bf16, out_bufs, out_sems):
                    put.wait()

            for fetch, put, ibuf, obuf in zip(_fetch_copies(s, layer, next_f32, in_bufs, in_sems),
                                              _put_copies(s, next_bf16, out_bufs, out_sems), in_bufs, out_bufs):
                fetch.wait()
                obuf[...] = ibuf[...].astype(_BF16)
                put.start()

        @pl.when(s == n_tiles)
        def _():
            for put in _put_copies(s - 1, next_bf16, out_bufs, out_sems):
                put.wait()


def _resident(shape, layer=None):
    if layer is None:
        return pl.BlockSpec(shape, lambda s: (0,) * len(shape), pipeline_mode=pl.Buffered(1))
    return pl.BlockSpec((None,) + shape, lambda s: (layer,) + (0,) * len(shape), pipeline_mode=pl.Buffered(1))


def _layer_call(x, layer, n1, win, convw, sinks, relb, poolw, pscale, wout, n2, w1, w2, bkt, fnorm, next_f32):
    batch, seq, _ = x.shape
    assert seq % TM == 0 and TM % BLOCK == 0
    tiles_per_seq = seq // TM
    n_tiles = batch * tiles_per_seq
    final = fnorm is not None
    cast_next = next_f32 is not None
    smem = pl.BlockSpec(memory_space=pltpu.SMEM)
    hbm = pl.BlockSpec(memory_space=pl.ANY)

    def in_tile(s):
        t = jnp.minimum(s, n_tiles - 1)
        return (t // tiles_per_seq, t % tiles_per_seq, 0)

    def out_tile(s):
        t = jnp.maximum(s - 1, 0)
        return (t // tiles_per_seq, t % tiles_per_seq, 0)

    in_specs = [
        pl.BlockSpec((1, TM, D_MODEL), in_tile),
        _resident((1, D_MODEL), layer),
        _resident((D_MODEL, IN_WIDTH)),
        _resident((CONV_K, CONV_WIDTH), layer),
        smem,
        smem,
        _resident((POOL_WIDTH, POOL_GROUP), layer),
        _resident((1, POOL_WIDTH), layer),
        _resident((D_MODEL, D_MODEL)),
        _resident((1, D_MODEL), layer),
        _resident((D_MODEL, D_FF)),
        _resident((D_FF, D_MODEL)),
        _resident((2 * BLOCK, BLOCK)),
    ]
    args = [x, n1, win, convw, sinks, relb, poolw, pscale, wout, n2, w1, w2, bkt]
    if final:
        in_specs.append(_resident((1, D_MODEL)))
        args.append(fnorm)
    out_specs = [pl.BlockSpec((1, TM, D_MODEL), out_tile)]
    out_shape = [jax.ShapeDtypeStruct(x.shape, x.dtype)]
    cast_scratch = []
    if cast_next:
        in_specs += [hbm] * len(next_f32)
        args += list(next_f32)
        out_specs += [hbm] * len(next_f32)
        out_shape += [jax.ShapeDtypeStruct(w.shape[1:], _BF16) for w in next_f32]
        rows = [w.shape[1] // n_tiles for w in next_f32]
        assert all(r * n_tiles == w.shape[1] and r % BF16_SUBLANES == 0 for r, w in zip(rows, next_f32))
        cast_scratch = ([pltpu.VMEM((r, w.shape[2]), _F32) for r, w in zip(rows, next_f32)]
                        + [pltpu.VMEM((r, w.shape[2]), _BF16) for r, w in zip(rows, next_f32)]
                        + [pltpu.SemaphoreType.DMA((len(next_f32),))] * 2)
    scratch = [
        pltpu.VMEM((2, TM, D_MODEL), _F32),
        pltpu.VMEM((2, TM, D_MODEL), _BF16),
        pltpu.VMEM((TM, D_FF), _BF16),
        pltpu.VMEM((TM, ATTN_WIDTH), _F32),
        pltpu.VMEM((KV_HALO + TM, KV_WIDTH), _BF16),
        pltpu.VMEM((KV_HALO + TM, KV_WIDTH), _F32),
        pltpu.VMEM((TM, _GATES_WIDTH), _F32),
        pltpu.VMEM((CONV_HALO + TM, CONV_WIDTH), _F32),
        pltpu.VMEM((POOL_HALO + TM, POOL_WIDTH), _F32),
        pltpu.VMEM((POOL_HALO + TM, POOL_WIDTH), _F32),
        pltpu.VMEM((POOL_HALO + TM, POOL_WIDTH), _F32),
        pltpu.VMEM((POOL_HALO + TM, POOL_WIDTH), _F32),
        pltpu.VMEM((TM, D_MODEL), _BF16),
        pltpu.VMEM((N_KV_HEADS, 2 * BLOCK, GROUP * BLOCK), _F32),
        pltpu.VMEM((POOL_WIDTH, POOL_WIDTH), _BF16),
    ] + cast_scratch
    outs = pl.pallas_call(
        functools.partial(_layer_kernel, layer=layer, final=final, cast_next=cast_next,
                          tiles_per_seq=tiles_per_seq),
        grid=(n_tiles + 1,),
        in_specs=in_specs,
        out_specs=out_specs,
        out_shape=out_shape,
        scratch_shapes=scratch,
        compiler_params=pltpu.CompilerParams(
            dimension_semantics=("arbitrary",),
            vmem_limit_bytes=V7X_VMEM_LIMIT_BYTES,
        ),
        name="hybrid_layer_final" if final else "hybrid_layer",
    )(*args)
    return outs[0], outs[1:]


def kernel(x, norm1, w_in, conv_w, sinks, pool_w, pool_scale, w_out, norm2, w1, w2, rel_bias, final_norm):
    depth = w_in.shape[0]
    bkt = jnp.asarray(_bucket_table())
    n1 = norm1.reshape(depth, 1, D_MODEL)
    n2 = norm2.reshape(depth, 1, D_MODEL)
    poolw = pool_w.reshape(depth, POOL_WIDTH, POOL_GROUP)
    pscale = pool_scale.reshape(depth, 1, POOL_WIDTH)
    weights_f32 = (w_in, w_out, w1, w2)
    win, wout, w1b, w2b = [w[0].astype(_BF16) for w in weights_f32]
    for l in range(depth):
        last = l == depth - 1
        fnorm = final_norm.reshape(1, D_MODEL) if last else None
        x, nxt = _layer_call(x, l, n1, win, conv_w, sinks, rel_bias, poolw, pscale, wout, n2, w1b, w2b, bkt, fnorm,
                             None if last else weights_f32)
        if not last:
            win, wout, w1b, w2b = nxt
    return x
```

```python
import functools
import math

import numpy as np
import jax
import jax.numpy as jnp
from jax import lax
from jax.experimental import pallas as pl
from jax.experimental.pallas import tpu as pltpu

D_MODEL = 1024
HEAD_DIM = 64
ATTN_WIDTH = 512
CONV_WIDTH = 256
CONV_K = 3
POOL_WIDTH = 256
N_Q_HEADS = 8
N_KV_HEADS = 2
GROUP = N_Q_HEADS // N_KV_HEADS
KV_WIDTH = N_KV_HEADS * HEAD_DIM
POOL_WINDOWS = (2, 4, 8, 16)
POOL_GROUP = 64
IN_WIDTH = 1792
D_FF = 4096
WINDOW = 128
BLOCK = 128
N_BUCKETS = 32
MAX_DISTANCE = 128
EPS = 1e-6
NEG = -1e30
LOG2E = math.log2(math.e)

_QKV_END = ATTN_WIDTH + 2 * KV_WIDTH
_GATES_WIDTH = IN_WIDTH - _QKV_END

TM = 512
KV_HALO = BLOCK
CONV_HALO = 8
POOL_HALO = 32
FF_PIECE = 512
BF16_SUBLANES = 16
V7X_VMEM_LIMIT_BYTES = 60 * 1024 * 1024
_PHASE_ORDER = "m x m x m x m x m x m x m x m x x x x x m x x m m".split()

_F32 = jnp.float32
_BF16 = jnp.bfloat16


def _bucket_table():
    kj = np.arange(2 * BLOCK, dtype=np.int32)[:, None]
    qi = np.arange(BLOCK, dtype=np.int32)[None, :] + BLOCK
    dist = qi - kj
    n = np.maximum(dist, 0)
    max_exact = N_BUCKETS // 2
    nf = np.maximum(n, 1).astype(np.float32)
    large = max_exact + (np.log(nf / np.float32(max_exact)) / np.float32(math.log(MAX_DISTANCE / max_exact))
                         * np.float32(N_BUCKETS - max_exact)).astype(np.int32)
    large = np.minimum(large, N_BUCKETS - 1)
    bucket = np.where(n < max_exact, n, large)
    valid = (dist >= 0) & (dist < WINDOW)
    return np.where(valid, bucket, -1).astype(np.int32)


def _rmsnorm(x, g):
    ms = jnp.mean(x * x, axis=-1, keepdims=True)
    return x * lax.rsqrt(ms + EPS) * g


def _build_tables(relb_ref, poolw_ref, bkt_ref, bias_scr, wbd_scr):
    bkt = bkt_ref[...]
    for h in range(N_Q_HEADS):
        acc = jnp.where(bkt < 0, NEG, 0.0).astype(_F32)
        for b in range(N_BUCKETS):
            acc = jnp.where(bkt == b, relb_ref[b, h] * LOG2E, acc)
        g, gi = divmod(h, GROUP)
        bias_scr[g, :, gi * BLOCK:(gi + 1) * BLOCK] = acc
    rows = lax.broadcasted_iota(jnp.int32, (POOL_GROUP, POOL_WIDTH), 0)
    cols = lax.broadcasted_iota(jnp.int32, (POOL_GROUP, POOL_WIDTH), 1)
    rep = jnp.where((cols % POOL_GROUP) == rows, 1.0, 0.0).astype(_BF16)
    tiled = jnp.dot(poolw_ref[...].astype(_BF16), rep, preferred_element_type=_F32)
    r2 = lax.broadcasted_iota(jnp.int32, (POOL_WIDTH, POOL_WIDTH), 0) // POOL_GROUP
    c2 = lax.broadcasted_iota(jnp.int32, (POOL_WIDTH, POOL_WIDTH), 1) // POOL_GROUP
    wbd_scr[...] = jnp.where(r2 == c2, tiled, 0.0).astype(_BF16)


def _attention_scores(n, g, first_tile, q_scr, k_scr, bias_scr, sink_row):
    r0 = n * BLOCK
    kband = k_scr[r0:r0 + 2 * BLOCK, :]
    zeros = jnp.zeros((HEAD_DIM, GROUP * BLOCK), _F32)
    scale = LOG2E / math.sqrt(HEAD_DIM)
    qg_t = (q_scr[r0:r0 + BLOCK, g * GROUP * HEAD_DIM:(g + 1) * GROUP * HEAD_DIM] * scale).T
    qcat = jnp.concatenate([qg_t[gi * HEAD_DIM:(gi + 1) * HEAD_DIM, :] for gi in range(GROUP)], axis=1)
    rhs = jnp.concatenate([qcat, zeros] if g == 0 else [zeros, qcat], axis=0).astype(_BF16)
    s_t = jnp.dot(kband, rhs, preferred_element_type=_F32) + bias_scr[g]
    if n == 0:
        s_t = jnp.concatenate([s_t[:BLOCK] + jnp.where(first_tile, NEG, 0.0), s_t[BLOCK:]], axis=0)
    m = jnp.maximum(jnp.max(s_t, axis=0, keepdims=True), sink_row)
    e = jnp.exp2(s_t - m)
    denom = jnp.sum(e, axis=0, keepdims=True) + jnp.exp2(sink_row - m)
    return e.astype(_BF16), 1.0 / denom


def _attention_output(n, g, e, inv_denom, v_scr, mixed_scr):
    r0 = n * BLOCK
    v_t = v_scr[r0:r0 + 2 * BLOCK, :].T[g * HEAD_DIM:(g + 1) * HEAD_DIM, :].astype(_BF16)
    o_t = jnp.dot(v_t, e, preferred_element_type=_F32) * inv_denom
    o4 = jnp.concatenate([o_t[:, gi * BLOCK:(gi + 1) * BLOCK] for gi in range(GROUP)], axis=0)
    mixed_scr[r0:r0 + BLOCK, g * GROUP * HEAD_DIM:(g + 1) * GROUP * HEAD_DIM] = o4.T.astype(_BF16)


def _mlp_half(x1_ref, h2_ref, w1_ref, w2_ref, fnorm_ref, o_ref, act_scr):
    for lo in range(0, D_FF, FF_PIECE):
        hid = jnp.dot(h2_ref[...], w1_ref[:, lo:lo + FF_PIECE], preferred_element_type=_F32)
        act_scr[:, lo:lo + FF_PIECE] = jnp.square(jnp.maximum(hid, 0.0)).astype(_BF16)
        yield
    halfd = D_MODEL // 2
    parts = []
    for i in range(2):
        parts.append(jnp.dot(act_scr[...], w2_ref[:, i * halfd:(i + 1) * halfd], preferred_element_type=_F32))
        yield
    out = x1_ref[...] + jnp.concatenate(parts, axis=1)
    if fnorm_ref is not None:
        out = _rmsnorm(out, fnorm_ref[...])
    o_ref[0] = out
    yield


def _mixer_half(x_ref, x1_ref, h2_ref, seq_tile, n1_ref, win_ref, convw_ref, sinks_ref, pscale_ref, wout_ref,
                n2_ref, layer, q_scr, k_scr, v_scr, g_scr, u_scr, p_scr, a1_scr, a2_scr, a3_scr, mixed_scr,
                bias_scr, wbd_scr):
    first_tile = seq_tile == 0

    x = x_ref[0]
    h = _rmsnorm(x, n1_ref[...]).astype(_BF16)
    q_scr[...] = jnp.dot(h, win_ref[:, 0:ATTN_WIDTH], preferred_element_type=_F32)
    for r in range(0, TM, TM // 2):
        kv = jnp.dot(h[r:r + TM // 2], win_ref[:, ATTN_WIDTH:_QKV_END], preferred_element_type=_F32)
        k_scr[KV_HALO + r:KV_HALO + r + TM // 2, :] = kv[:, 0:KV_WIDTH].astype(_BF16)
        v_scr[KV_HALO + r:KV_HALO + r + TM // 2, :] = kv[:, KV_WIDTH:2 * KV_WIDTH]
    yield

    lane = lax.broadcasted_iota(jnp.int32, (1, GROUP * BLOCK), 1)
    sink_rows = []
    for g in range(N_KV_HEADS):
        row = jnp.full((1, GROUP * BLOCK), sinks_ref[layer, g * GROUP + GROUP - 1], _F32)
        for gi in range(GROUP - 2, -1, -1):
            row = jnp.where(lane < (gi + 1) * BLOCK, sinks_ref[layer, g * GROUP + gi], row)
        sink_rows.append(row * LOG2E)
    gates_half = _GATES_WIDTH // 2
    n_pairs = (TM // BLOCK) * N_KV_HEADS
    pending = None
    for k in range(n_pairs):
        n, g = divmod(k, N_KV_HEADS)
        if pending is not None:
            _attention_output(*pending, v_scr, mixed_scr)
        e, inv_denom = _attention_scores(n, g, first_tile, q_scr, k_scr, bias_scr, sink_rows[g])
        pending = (n, g, e, inv_denom)
        yield
        i = k - (n_pairs - 2)
        if i >= 0:
            lo = _QKV_END + i * gates_half
            g_scr[:, i * gates_half:(i + 1) * gates_half] = jnp.dot(h, win_ref[:, lo:lo + gates_half],
                                                                    preferred_element_type=_F32)
            yield
    _attention_output(*pending, v_scr, mixed_scr)
    k_scr[0:KV_HALO, :] = k_scr[TM:TM + KV_HALO, :]
    v_scr[0:KV_HALO, :] = v_scr[TM:TM + KV_HALO, :]

    c0 = CONV_HALO
    u_scr[c0:c0 + TM, :] = g_scr[:, CONV_WIDTH:2 * CONV_WIDTH] * g_scr[:, 2 * CONV_WIDTH:3 * CONV_WIDTH]
    cw = convw_ref[...]
    y = (cw[0:1, :] * u_scr[c0 - 2:c0 - 2 + TM, :] + cw[1:2, :] * u_scr[c0 - 1:c0 - 1 + TM, :]
         + cw[2:3, :] * u_scr[c0:c0 + TM, :])
    mixed_scr[:, ATTN_WIDTH:ATTN_WIDTH + CONV_WIDTH] = (g_scr[:, 0:CONV_WIDTH] * y).astype(_BF16)
    u_scr[0:CONV_HALO, :] = u_scr[TM:TM + CONV_HALO, :]

    p0 = POOL_HALO
    pe = p0 + TM
    p = g_scr[:, 3 * CONV_WIDTH:_GATES_WIDTH]
    p_scr[p0:pe, :] = p
    a1_scr[8:pe, :] = p_scr[8:pe, :] + p_scr[7:pe - 1, :]
    a2_scr[16:pe, :] = a1_scr[16:pe, :] + a1_scr[14:pe - 2, :]
    a3_scr[24:pe, :] = a2_scr[24:pe, :] + a2_scr[20:pe - 4, :]
    s16 = a3_scr[p0:pe, :] + a3_scr[p0 - 8:pe - 8, :]
    plane = lax.broadcasted_iota(jnp.int32, (TM, POOL_WIDTH), 1)
    ssum = jnp.where(plane < POOL_GROUP, a1_scr[p0:pe, :],
                     jnp.where(plane < 2 * POOL_GROUP, a2_scr[p0:pe, :],
                               jnp.where(plane < 3 * POOL_GROUP, a3_scr[p0:pe, :], s16)))
    win = jnp.where(plane < POOL_GROUP, POOL_WINDOWS[0],
                    jnp.where(plane < 2 * POOL_GROUP, POOL_WINDOWS[1],
                              jnp.where(plane < 3 * POOL_GROUP, POOL_WINDOWS[2], POOL_WINDOWS[3])))
    t = seq_tile * TM + lax.broadcasted_iota(jnp.int32, (TM, POOL_WIDTH), 0)
    count = jnp.minimum(t + 1, win).astype(_F32)
    pooled = (ssum / count - p).astype(_BF16)
    p_scr[0:POOL_HALO, :] = p_scr[TM:TM + POOL_HALO, :]
    yield
    for r in range(0, TM, TM // 2):
        mixed_pool = jnp.dot(pooled[r:r + TM // 2], wbd_scr[...], preferred_element_type=_F32) * pscale_ref[...]
        mixed_scr[r:r + TM // 2, ATTN_WIDTH + CONV_WIDTH:D_MODEL] = mixed_pool.astype(_BF16)
    yield

    x1 = x + jnp.dot(mixed_scr[...], wout_ref[...], preferred_element_type=_F32)
    x1_ref[...] = x1
    h2_ref[...] = _rmsnorm(x1, n2_ref[...]).astype(_BF16)
    yield


def _step_rows(s, buf):
    rows = buf.shape[0]
    return pl.ds(pl.multiple_of(s * rows, rows), rows)


def _fetch_copies(s, layer, srcs, in_bufs, sems):
    return [pltpu.make_async_copy(src.at[layer + 1, _step_rows(s, buf), :], buf, sems.at[i])
            for i, (src, buf) in enumerate(zip(srcs, in_bufs))]


def _put_copies(s, dsts, out_bufs, sems):
    return [pltpu.make_async_copy(buf, dst.at[_step_rows(s, buf), :], sems.at[i])
            for i, (dst, buf) in enumerate(zip(dsts, out_bufs))]


def _layer_kernel(x_ref, n1_ref, win_ref, convw_ref, sinks_ref, relb_ref, poolw_ref, pscale_ref, wout_ref,
                  n2_ref, w1_ref, w2_ref, bkt_ref, *rest, layer, final, cast_next, tiles_per_seq):
    rest = list(rest)
    fnorm_ref = rest.pop(0) if final else None
    next_f32 = [rest.pop(0) for _ in range(4)] if cast_next else []
    o_ref = rest.pop(0)
    next_bf16 = [rest.pop(0) for _ in range(4)] if cast_next else []
    (x1_scr, h2_scr, act_scr, q_scr, k_scr, v_scr, g_scr, u_scr, p_scr, a1_scr, a2_scr, a3_scr, mixed_scr,
     bias_scr, wbd_scr) = rest[:15]
    s = pl.program_id(0)
    n_tiles = pl.num_programs(0) - 1
    slot = s % 2
    seq_tile = jnp.minimum(s, n_tiles - 1) % tiles_per_seq

    if cast_next:
        in_bufs, out_bufs, (in_sems, out_sems) = rest[15:19], rest[19:23], rest[23:25]

        @pl.when(s < n_tiles)
        def _():
            for fetch in _fetch_copies(s, layer, next_f32, in_bufs, in_sems):
                fetch.start()

    @pl.when(s == 0)
    def _():
        _build_tables(relb_ref, poolw_ref, bkt_ref, bias_scr, wbd_scr)

    @pl.when(jnp.logical_and(seq_tile == 0, s < n_tiles))
    def _():
        k_scr[0:KV_HALO, :] = jnp.zeros((KV_HALO, KV_WIDTH), _BF16)
        v_scr[0:KV_HALO, :] = jnp.zeros((KV_HALO, KV_WIDTH), _F32)
        u_scr[0:CONV_HALO, :] = jnp.zeros((CONV_HALO, CONV_WIDTH), _F32)
        p_scr[0:POOL_HALO, :] = jnp.zeros((POOL_HALO, POOL_WIDTH), _F32)

    def mlp_half():
        return _mlp_half(x1_scr.at[1 - slot], h2_scr.at[1 - slot], w1_ref, w2_ref, fnorm_ref if final else None,
                         o_ref, act_scr)

    def mixer_half():
        return _mixer_half(x_ref, x1_scr.at[slot], h2_scr.at[slot], seq_tile, n1_ref, win_ref, convw_ref,
                           sinks_ref, pscale_ref, wout_ref, n2_ref, layer, q_scr, k_scr, v_scr, g_scr, u_scr, p_scr,
                           a1_scr, a2_scr, a3_scr, mixed_scr, bias_scr, wbd_scr)

    @pl.when(s == 0)
    def _():
        for _ in mixer_half():
            pass

    @pl.when(jnp.logical_and(s > 0, s < n_tiles))
    def _():
        mlp, mixer = mlp_half(), mixer_half()
        for who in _PHASE_ORDER:
            next(mlp if who == "m" else mixer)
        for half in (mlp, mixer):
            assert next(half, "done") == "done", "phase order does not cover every phase"

    @pl.when(s == n_tiles)
    def _():
        for _ in mlp_half():
            pass

    if cast_next:
        @pl.when(s < n_tiles)
        def _():
            @pl.when(s > 0)
            def _():
                for put in _put_copies(s - 1, next_bf16, out_bufs, out_sems):
                    put.wait()

            for fetch, put, ibuf, obuf in zip(_fetch_copies(s, layer, next_f32, in_bufs, in_sems),
                                              _put_copies(s, next_bf16, out_bufs, out_sems), in_bufs, out_bufs):
                fetch.wait()
                obuf[...] = ibuf[...].astype(_BF16)
                put.start()

        @pl.when(s == n_tiles)
        def _():
            for put in _put_copies(s - 1, next_bf16, out_bufs, out_sems):
                put.wait()


def _resident(shape, layer=None):
    if layer is None:
        return pl.BlockSpec(shape, lambda s: (0,) * len(shape), pipeline_mode=pl.Buffered(1))
    return pl.BlockSpec((None,) + shape, lambda s: (layer,) + (0,) * len(shape), pipeline_mode=pl.Buffered(1))


def _layer_call(x, layer, n1, win, convw, sinks, relb, poolw, pscale, wout, n2, w1, w2, bkt, fnorm, next_f32):
    batch, seq, _ = x.shape
    assert seq % TM == 0 and TM % BLOCK == 0
    tiles_per_seq = seq // TM
    n_tiles = batch * tiles_per_seq
    final = fnorm is not None
    cast_next = next_f32 is not None
    smem = pl.BlockSpec(memory_space=pltpu.SMEM)
    hbm = pl.BlockSpec(memory_space=pl.ANY)

    def in_tile(s):
        t = jnp.minimum(s, n_tiles - 1)
        return (t // tiles_per_seq, t % tiles_per_seq, 0)

    def out_tile(s):
        t = jnp.maximum(s - 1, 0)
        return (t // tiles_per_seq, t % tiles_per_seq, 0)

    in_specs = [
        pl.BlockSpec((1, TM, D_MODEL), in_tile),
        _resident((1, D_MODEL), layer),
        _resident((D_MODEL, IN_WIDTH)),
        _resident((CONV_K, CONV_WIDTH), layer),
        smem,
        smem,
        _resident((POOL_WIDTH, POOL_GROUP), layer),
        _resident((1, POOL_WIDTH), layer),
        _resident((D_MODEL, D_MODEL)),
        _resident((1, D_MODEL), layer),
        _resident((D_MODEL, D_FF)),
        _resident((D_FF, D_MODEL)),
        _resident((2 * BLOCK, BLOCK)),
    ]
    args = [x, n1, win, convw, sinks, relb, poolw, pscale, wout, n2, w1, w2, bkt]
    if final:
        in_specs.append(_resident((1, D_MODEL)))
        args.append(fnorm)
    out_specs = [pl.BlockSpec((1, TM, D_MODEL), out_tile)]
    out_shape = [jax.ShapeDtypeStruct(x.shape, x.dtype)]
    cast_scratch = []
    if cast_next:
        in_specs += [hbm] * len(next_f32)
        args += list(next_f32)
        out_specs += [hbm] * len(next_f32)
        out_shape += [jax.ShapeDtypeStruct(w.shape[1:], _BF16) for w in next_f32]
        rows = [w.shape[1] // n_tiles for w in next_f32]
        assert all(r * n_tiles == w.shape[1] and r % BF16_SUBLANES == 0 for r, w in zip(rows, next_f32))
        cast_scratch = ([pltpu.VMEM((r, w.shape[2]), _F32) for r, w in zip(rows, next_f32)]
                        + [pltpu.VMEM((r, w.shape[2]), _BF16) for r, w in zip(rows, next_f32)]
                        + [pltpu.SemaphoreType.DMA((len(next_f32),))] * 2)
    scratch = [
        pltpu.VMEM((2, TM, D_MODEL), _F32),
        pltpu.VMEM((2, TM, D_MODEL), _BF16),
        pltpu.VMEM((TM, D_FF), _BF16),
        pltpu.VMEM((TM, ATTN_WIDTH), _F32),
        pltpu.VMEM((KV_HALO + TM, KV_WIDTH), _BF16),
        pltpu.VMEM((KV_HALO + TM, KV_WIDTH), _F32),
        pltpu.VMEM((TM, _GATES_WIDTH), _F32),
        pltpu.VMEM((CONV_HALO + TM, CONV_WIDTH), _F32),
        pltpu.VMEM((POOL_HALO + TM, POOL_WIDTH), _F32),
        pltpu.VMEM((POOL_HALO + TM, POOL_WIDTH), _F32),
        pltpu.VMEM((POOL_HALO + TM, POOL_WIDTH), _F32),
        pltpu.VMEM((POOL_HALO + TM, POOL_WIDTH), _F32),
        pltpu.VMEM((TM, D_MODEL), _BF16),
        pltpu.VMEM((N_KV_HEADS, 2 * BLOCK, GROUP * BLOCK), _F32),
        pltpu.VMEM((POOL_WIDTH, POOL_WIDTH), _BF16),
    ] + cast_scratch
    outs = pl.pallas_call(
        functools.partial(_layer_kernel, layer=layer, final=final, cast_next=cast_next,
                          tiles_per_seq=tiles_per_seq),
        grid=(n_tiles + 1,),
        in_specs=in_specs,
        out_specs=out_specs,
        out_shape=out_shape,
        scratch_shapes=scratch,
        compiler_params=pltpu.CompilerParams(
            dimension_semantics=("arbitrary",),
            vmem_limit_bytes=V7X_VMEM_LIMIT_BYTES,
        ),
        name="hybrid_layer_final" if final else "hybrid_layer",
    )(*args)
    return outs[0], outs[1:]


def kernel(x, norm1, w_in, conv_w, sinks, pool_w, pool_scale, w_out, norm2, w1, w2, rel_bias, final_norm):
    depth = w_in.shape[0]
    bkt = jnp.asarray(_bucket_table())
    n1 = norm1.reshape(depth, 1, D_MODEL)
    n2 = norm2.reshape(depth, 1, D_MODEL)
    poolw = pool_w.reshape(depth, POOL_WIDTH, POOL_GROUP)
    pscale = pool_scale.reshape(depth, 1, POOL_WIDTH)
    weights_f32 = (w_in, w_out, w1, w2)
    win, wout, w1b, w2b = [w[0].astype(_BF16) for w in weights_f32]
    for l in range(depth):
        last = l == depth - 1
        fnorm = final_norm.reshape(1, D_MODEL) if last else None
        x, nxt = _layer_call(x, l, n1, win, conv_w, sinks, rel_bias, poolw, pscale, wout, n2, w1b, w2b, bkt, fnorm,
                             None if last else weights_f32)
        if not last:
            win, wout, w1b, w2b = nxt
    return x
```

```python
import functools
import math

import numpy as np
import jax
import jax.numpy as jnp
from jax import lax
from jax.experimental import pallas as pl
from jax.experimental.pallas import tpu as pltpu

D_MODEL = 1024
HEAD_DIM = 64
ATTN_WIDTH = 512
CONV_WIDTH = 256
CONV_K = 3
POOL_WIDTH = 256
N_Q_HEADS = 8
N_KV_HEADS = 2
GROUP = N_Q_HEADS // N_KV_HEADS
KV_WIDTH = N_KV_HEADS * HEAD_DIM
POOL_WINDOWS = (2, 4, 8, 16)
POOL_GROUP = 64
IN_WIDTH = 1792
D_FF = 4096
WINDOW = 128
BLOCK = 128
N_BUCKETS = 32
MAX_DISTANCE = 128
EPS = 1e-6
NEG = -1e30
LOG2E = math.log2(math.e)

_QKV_END = ATTN_WIDTH + 2 * KV_WIDTH
_GATES_WIDTH = IN_WIDTH - _QKV_END

TM = 512
KV_HALO = BLOCK
CONV_HALO = 8
POOL_HALO = 32
FF_PIECE = 512
BF16_SUBLANES = 16
V7X_VMEM_LIMIT_BYTES = 60 * 1024 * 1024
_PHASE_ORDER = "m x m x m x m x m x m x m x m x x x x x m x x m m".split()

_F32 = jnp.float32
_BF16 = jnp.bfloat16


def _bucket_table():
    kj = np.arange(2 * BLOCK, dtype=np.int32)[:, None]
    qi = np.arange(BLOCK, dtype=np.int32)[None, :] + BLOCK
    dist = qi - kj
    n = np.maximum(dist, 0)
    max_exact = N_BUCKETS // 2
    nf = np.maximum(n, 1).astype(np.float32)
    large = max_exact + (np.log(nf / np.float32(max_exact)) / np.float32(math.log(MAX_DISTANCE / max_exact))
                         * np.float32(N_BUCKETS - max_exact)).astype(np.int32)
    large = np.minimum(large, N_BUCKETS - 1)
    bucket = np.where(n < max_exact, n, large)
    valid = (dist >= 0) & (dist < WINDOW)
    return np.where(valid, bucket, -1).astype(np.int32)


def _rmsnorm(x, g):
    ms = jnp.mean(x * x, axis=-1, keepdims=True)
    return x * lax.rsqrt(ms + EPS) * g


def _tables_kernel(relb_ref, poolw_ref, bkt_ref, bias_ref, wbd_ref):
    bkt = bkt_ref[...]
    for h in range(N_Q_HEADS):
        acc = jnp.where(bkt < 0, NEG, 0.0).astype(_F32)
        for b in range(N_BUCKETS):
            acc = jnp.where(bkt == b, relb_ref[b, h] * LOG2E, acc)
        g, gi = divmod(h, GROUP)
        bias_ref[g, :, gi * BLOCK:(gi + 1) * BLOCK] = acc
    rows = lax.broadcasted_iota(jnp.int32, (POOL_GROUP, POOL_WIDTH), 0)
    cols = lax.broadcasted_iota(jnp.int32, (POOL_GROUP, POOL_WIDTH), 1)
    rep = jnp.where((cols % POOL_GROUP) == rows, 1.0, 0.0).astype(_BF16)
    r2 = lax.broadcasted_iota(jnp.int32, (POOL_WIDTH, POOL_WIDTH), 0) // POOL_GROUP
    c2 = lax.broadcasted_iota(jnp.int32, (POOL_WIDTH, POOL_WIDTH), 1) // POOL_GROUP
    for l in range(poolw_ref.shape[0]):
        tiled = jnp.dot(poolw_ref[l].astype(_BF16), rep, preferred_element_type=_F32)
        wbd_ref[l] = jnp.where(r2 == c2, tiled, 0.0).astype(_BF16)


def _tables_call(relb, poolw, bkt):
    depth = poolw.shape[0]
    vmem = pl.BlockSpec(memory_space=pltpu.VMEM)
    return pl.pallas_call(
        _tables_kernel,
        in_specs=[pl.BlockSpec(memory_space=pltpu.SMEM), vmem, vmem],
        out_specs=[vmem, vmem],
        out_shape=[jax.ShapeDtypeStruct((N_KV_HEADS, 2 * BLOCK, GROUP * BLOCK), _F32),
                   jax.ShapeDtypeStruct((depth, POOL_WIDTH, POOL_WIDTH), _BF16)],
        name="hybrid_tables",
    )(relb, poolw, bkt)


def _attention_scores(n, g, first_tile, q_scr, k_scr, bias_scr, sink_row):
    r0 = n * BLOCK
    kband = k_scr[r0:r0 + 2 * BLOCK, :]
    zeros = jnp.zeros((HEAD_DIM, GROUP * BLOCK), _F32)
    scale = LOG2E / math.sqrt(HEAD_DIM)
    qg_t = (q_scr[r0:r0 + BLOCK, g * GROUP * HEAD_DIM:(g + 1) * GROUP * HEAD_DIM] * scale).T
    qcat = jnp.concatenate([qg_t[gi * HEAD_DIM:(gi + 1) * HEAD_DIM, :] for gi in range(GROUP)], axis=1)
    rhs = jnp.concatenate([qcat, zeros] if g == 0 else [zeros, qcat], axis=0).astype(_BF16)
    s_t = jnp.dot(kband, rhs, preferred_element_type=_F32) + bias_scr[g]
    if n == 0:
        s_t = jnp.concatenate([s_t[:BLOCK] + jnp.where(first_tile, NEG, 0.0), s_t[BLOCK:]], axis=0)
    m = jnp.maximum(jnp.max(s_t, axis=0, keepdims=True), sink_row)
    e = jnp.exp2(s_t - m)
    denom = jnp.sum(e, axis=0, keepdims=True) + jnp.exp2(sink_row - m)
    return e.astype(_BF16), 1.0 / denom


def _attention_output(n, g, e, inv_denom, v_scr, mixed_scr):
    r0 = n * BLOCK
    v_t = v_scr[r0:r0 + 2 * BLOCK, :].T[g * HEAD_DIM:(g + 1) * HEAD_DIM, :].astype(_BF16)
    o_t = jnp.dot(v_t, e, preferred_element_type=_F32) * inv_denom
    o4 = jnp.concatenate([o_t[:, gi * BLOCK:(gi + 1) * BLOCK] for gi in range(GROUP)], axis=0)
    mixed_scr[r0:r0 + BLOCK, g * GROUP * HEAD_DIM:(g + 1) * GROUP * HEAD_DIM] = o4.T.astype(_BF16)


def _mlp_half(x1_ref, h2_ref, w1_ref, w2_ref, fnorm_ref, o_ref, act_scr):
    for lo in range(0, D_FF, FF_PIECE):
        hid = jnp.dot(h2_ref[...], w1_ref[:, lo:lo + FF_PIECE], preferred_element_type=_F32)
        act_scr[:, lo:lo + FF_PIECE] = jnp.square(jnp.maximum(hid, 0.0)).astype(_BF16)
        yield
    halfd = D_MODEL // 2
    parts = []
    for i in range(2):
        parts.append(jnp.dot(act_scr[...], w2_ref[:, i * halfd:(i + 1) * halfd], preferred_element_type=_F32))
        yield
    out = x1_ref[...] + jnp.concatenate(parts, axis=1)
    if fnorm_ref is not None:
        out = _rmsnorm(out, fnorm_ref[...])
    o_ref[0] = out
    yield


def _mixer_half(x_ref, x1_ref, h2_ref, seq_tile, n1_ref, win_ref, convw_ref, sinks_ref, pscale_ref, wout_ref,
                n2_ref, layer, q_scr, k_scr, v_scr, g_scr, u_scr, p_scr, a1_scr, a2_scr, a3_scr, mixed_scr,
                bias_scr, wbd_scr):
    first_tile = seq_tile == 0

    x = x_ref[0]
    h = _rmsnorm(x, n1_ref[...]).astype(_BF16)
    q_scr[...] = jnp.dot(h, win_ref[:, 0:ATTN_WIDTH], preferred_element_type=_F32)
    for r in range(0, TM, TM // 2):
        kv = jnp.dot(h[r:r + TM // 2], win_ref[:, ATTN_WIDTH:_QKV_END], preferred_element_type=_F32)
        k_scr[KV_HALO + r:KV_HALO + r + TM // 2, :] = kv[:, 0:KV_WIDTH].astype(_BF16)
        v_scr[KV_HALO + r:KV_HALO + r + TM // 2, :] = kv[:, KV_WIDTH:2 * KV_WIDTH]
    yield

    lane = lax.broadcasted_iota(jnp.int32, (1, GROUP * BLOCK), 1)
    sink_rows = []
    for g in range(N_KV_HEADS):
        row = jnp.full((1, GROUP * BLOCK), sinks_ref[layer, g * GROUP + GROUP - 1], _F32)
        for gi in range(GROUP - 2, -1, -1):
            row = jnp.where(lane < (gi + 1) * BLOCK, sinks_ref[layer, g * GROUP + gi], row)
        sink_rows.append(row * LOG2E)
    gates_half = _GATES_WIDTH // 2
    n_pairs = (TM // BLOCK) * N_KV_HEADS
    pending = None
    for k in range(n_pairs):
        n, g = divmod(k, N_KV_HEADS)
        if pending is not None:
            _attention_output(*pending, v_scr, mixed_scr)
        e, inv_denom = _attention_scores(n, g, first_tile, q_scr, k_scr, bias_scr, sink_rows[g])
        pending = (n, g, e, inv_denom)
        yield
        i = k - (n_pairs - 2)
        if i >= 0:
            lo = _QKV_END + i * gates_half
            g_scr[:, i * gates_half:(i + 1) * gates_half] = jnp.dot(h, win_ref[:, lo:lo + gates_half],
                                                                    preferred_element_type=_F32)
            yield
    _attention_output(*pending, v_scr, mixed_scr)
    k_scr[0:KV_HALO, :] = k_scr[TM:TM + KV_HALO, :]
    v_scr[0:KV_HALO, :] = v_scr[TM:TM + KV_HALO, :]

    c0 = CONV_HALO
    u_scr[c0:c0 + TM, :] = g_scr[:, CONV_WIDTH:2 * CONV_WIDTH] * g_scr[:, 2 * CONV_WIDTH:3 * CONV_WIDTH]
    cw = convw_ref[...]
    y = (cw[0:1, :] * u_scr[c0 - 2:c0 - 2 + TM, :] + cw[1:2, :] * u_scr[c0 - 1:c0 - 1 + TM, :]
         + cw[2:3, :] * u_scr[c0:c0 + TM, :])
    mixed_scr[:, ATTN_WIDTH:ATTN_WIDTH + CONV_WIDTH] = (g_scr[:, 0:CONV_WIDTH] * y).astype(_BF16)
    u_scr[0:CONV_HALO, :] = u_scr[TM:TM + CONV_HALO, :]

    p0 = POOL_HALO
    pe = p0 + TM
    p = g_scr[:, 3 * CONV_WIDTH:_GATES_WIDTH]
    p_scr[p0:pe, :] = p
    a1_scr[8:pe, :] = p_scr[8:pe, :] + p_scr[7:pe - 1, :]
    a2_scr[16:pe, :] = a1_scr[16:pe, :] + a1_scr[14:pe - 2, :]
    a3_scr[24:pe, :] = a2_scr[24:pe, :] + a2_scr[20:pe - 4, :]
    s16 = a3_scr[p0:pe, :] + a3_scr[p0 - 8:pe - 8, :]
    plane = lax.broadcasted_iota(jnp.int32, (TM, POOL_WIDTH), 1)
    ssum = jnp.where(plane < POOL_GROUP, a1_scr[p0:pe, :],
                     jnp.where(plane < 2 * POOL_GROUP, a2_scr[p0:pe, :],
                               jnp.where(plane < 3 * POOL_GROUP, a3_scr[p0:pe, :], s16)))
    win = jnp.where(plane < POOL_GROUP, POOL_WINDOWS[0],
                    jnp.where(plane < 2 * POOL_GROUP, POOL_WINDOWS[1],
                              jnp.where(plane < 3 * POOL_GROUP, POOL_WINDOWS[2], POOL_WINDOWS[3])))
    t = seq_tile * TM + lax.broadcasted_iota(jnp.int32, (TM, POOL_WIDTH), 0)
    count = jnp.minimum(t + 1, win).astype(_F32)
    pooled = (ssum / count - p).astype(_BF16)
    p_scr[0:POOL_HALO, :] = p_scr[TM:TM + POOL_HALO, :]
    yield
    for r in range(0, TM, TM // 2):
        mixed_pool = jnp.dot(pooled[r:r + TM // 2], wbd_scr[...], preferred_element_type=_F32) * pscale_ref[...]
        mixed_scr[r:r + TM // 2, ATTN_WIDTH + CONV_WIDTH:D_MODEL] = mixed_pool.astype(_BF16)
    yield

    x1 = x + jnp.dot(mixed_scr[...], wout_ref[...], preferred_element_type=_F32)
    x1_ref[...] = x1
    h2_ref[...] = _rmsnorm(x1, n2_ref[...]).astype(_BF16)
    yield


def _step_rows(s, buf):
    rows = buf.shape[0]
    return pl.ds(pl.multiple_of(s * rows, rows), rows)


def _fetch_copies(s, layer, srcs, in_bufs, sems):
    return [pltpu.make_async_copy(src.at[layer + 1, _step_rows(s, buf), :], buf, sems.at[i])
            for i, (src, buf) in enumerate(zip(srcs, in_bufs))]


def _put_copies(s, dsts, out_bufs, sems):
    return [pltpu.make_async_copy(buf, dst.at[_step_rows(s, buf), :], sems.at[i])
            for i, (dst, buf) in enumerate(zip(dsts, out_bufs))]


_N_BASE_SCRATCH = 16


def _layer_kernel(x_ref, n1_ref, win_ref, convw_ref, sinks_ref, bias_scr, wbd_scr, pscale_ref, wout_ref,
                  n2_ref, w1_hbm, w2_hbm, *rest, layer, final, cast_next, tiles_per_seq):
    rest = list(rest)
    fnorm_ref = rest.pop(0) if final else None
    next_f32 = [rest.pop(0) for _ in range(4)] if cast_next else []
    o_ref = rest.pop(0)
    next_bf16 = [rest.pop(0) for _ in range(4)] if cast_next else []
    (x1_scr, h2_scr, act_scr, q_scr, k_scr, v_scr, g_scr, u_scr, p_scr, a1_scr, a2_scr, a3_scr, mixed_scr,
     w1_ref, w2_ref, mlp_w_sems) = rest[:_N_BASE_SCRATCH]
    s = pl.program_id(0)
    n_tiles = pl.num_programs(0) - 1
    slot = s % 2
    seq_tile = jnp.minimum(s, n_tiles - 1) % tiles_per_seq

    mlp_w_copies = [pltpu.make_async_copy(src, dst, mlp_w_sems.at[i])
                    for i, (src, dst) in enumerate(((w1_hbm, w1_ref), (w2_hbm, w2_ref)))]

    @pl.when(s == 0)
    def _():
        for copy in mlp_w_copies:
            copy.start()

    @pl.when(s == 1)
    def _():
        for copy in mlp_w_copies:
            copy.wait()

    if cast_next:
        cast_scratch = rest[_N_BASE_SCRATCH:]
        in_bufs, out_bufs, (in_sems, out_sems) = cast_scratch[0:4], cast_scratch[4:8], cast_scratch[8:10]

        @pl.when(s < n_tiles)
        def _():
            for fetch in _fetch_copies(s, layer, next_f32, in_bufs, in_sems):
                fetch.start()

    @pl.when(jnp.logical_and(seq_tile == 0, s < n_tiles))
    def _():
        k_scr[0:KV_HALO, :] = jnp.zeros((KV_HALO, KV_WIDTH), _BF16)
        v_scr[0:KV_HALO, :] = jnp.zeros((KV_HALO, KV_WIDTH), _F32)
        u_scr[0:CONV_HALO, :] = jnp.zeros((CONV_HALO, CONV_WIDTH), _F32)
        p_scr[0:POOL_HALO, :] = jnp.zeros((POOL_HALO, POOL_WIDTH), _F32)

    def mlp_half():
        return _mlp_half(x1_scr.at[1 - slot], h2_scr.at[1 - slot], w1_ref, w2_ref, fnorm_ref if final else None,
                         o_ref, act_scr)

    def mixer_half():
        return _mixer_half(x_ref, x1_scr.at[slot], h2_scr.at[slot], seq_tile, n1_ref, win_ref, convw_ref,
                           sinks_ref, pscale_ref, wout_ref, n2_ref, layer, q_scr, k_scr, v_scr, g_scr, u_scr, p_scr,
                           a1_scr, a2_scr, a3_scr, mixed_scr, bias_scr, wbd_scr)

    @pl.when(s == 0)
    def _():
        for _ in mixer_half():
            pass

    @pl.when(jnp.logical_and(s > 0, s < n_tiles))
    def _():
        mlp, mixer = mlp_half(), mixer_half()
        for who in _PHASE_ORDER:
            next(mlp if who == "m" else mixer)
        for half in (mlp, mixer):
            assert next(half, "done") == "done", "phase order does not cover every phase"

    @pl.when(s == n_tiles)
    def _():
        for _ in mlp_half():
            pass

    if cast_next:
        @pl.when(s < n_tiles)
        def _():
            @pl.when(s > 0)
            def _():
                for put in _put_copies(s - 1, next_bf16, out_bufs, out_sems):
                    put.wait()

            for fetch, put, ibuf, obuf in zip(_fetch_copies(s, layer, next_f32, in_bufs, in_sems),
                                              _put_copies(s, next_bf16, out_bufs, out_sems), in_bufs, out_bufs):
                fetch.wait()
                obuf[...] = ibuf[...].astype(_BF16)
                put.start()

        @pl.when(s == n_tiles)
        def _():
            for put in _put_copies(s - 1, next_bf16, out_bufs, out_sems):
                put.wait()


def _resident(shape, layer=None):
    if layer is None:
        return pl.BlockSpec(shape, lambda s: (0,) * len(shape), pipeline_mode=pl.Buffered(1))
    return pl.BlockSpec((None,) + shape, lambda s: (layer,) + (0,) * len(shape), pipeline_mode=pl.Buffered(1))


def _layer_call(x, layer, n1, win, convw, sinks, bias, wbd, pscale, wout, n2, w1, w2, fnorm, next_f32):
    batch, seq, _ = x.shape
    assert seq % TM == 0 and TM % BLOCK == 0
    tiles_per_seq = seq // TM
    n_tiles = batch * tiles_per_seq
    final = fnorm is not None
    cast_next = next_f32 is not None
    smem = pl.BlockSpec(memory_space=pltpu.SMEM)
    hbm = pl.BlockSpec(memory_space=pl.ANY)

    def in_tile(s):
        t = jnp.minimum(s, n_tiles - 1)
        return (t // tiles_per_seq, t % tiles_per_seq, 0)

    def out_tile(s):
        t = jnp.maximum(s - 1, 0)
        return (t // tiles_per_seq, t % tiles_per_seq, 0)

    in_specs = [
        pl.BlockSpec((1, TM, D_MODEL), in_tile),
        _resident((1, D_MODEL), layer),
        _resident((D_MODEL, IN_WIDTH)),
        _resident((CONV_K, CONV_WIDTH), layer),
        smem,
        _resident((N_KV_HEADS, 2 * BLOCK, GROUP * BLOCK)),
        _resident((POOL_WIDTH, POOL_WIDTH), layer),
        _resident((1, POOL_WIDTH), layer),
        _resident((D_MODEL, D_MODEL)),
        _resident((1, D_MODEL), layer),
        hbm,
        hbm,
    ]
    args = [x, n1, win, convw, sinks, bias, wbd, pscale, wout, n2, w1, w2]
    if final:
        in_specs.append(_resident((1, D_MODEL)))
        args.append(fnorm)
    out_specs = [pl.BlockSpec((1, TM, D_MODEL), out_tile)]
    out_shape = [jax.ShapeDtypeStruct(x.shape, x.dtype)]
    cast_scratch = []
    if cast_next:
        in_specs += [hbm] * len(next_f32)
        args += list(next_f32)
        out_specs += [hbm] * len(next_f32)
        out_shape += [jax.ShapeDtypeStruct(w.shape[1:], _BF16) for w in next_f32]
        rows = [w.shape[1] // n_tiles for w in next_f32]
        assert all(r * n_tiles == w.shape[1] and r % BF16_SUBLANES == 0 for r, w in zip(rows, next_f32))
        cast_scratch = ([pltpu.VMEM((r, w.shape[2]), _F32) for r, w in zip(rows, next_f32)]
                        + [pltpu.VMEM((r, w.shape[2]), _BF16) for r, w in zip(rows, next_f32)]
                        + [pltpu.SemaphoreType.DMA((len(next_f32),))] * 2)
    scratch = [
        pltpu.VMEM((2, TM, D_MODEL), _F32),
        pltpu.VMEM((2, TM, D_MODEL), _BF16),
        pltpu.VMEM((TM, D_FF), _BF16),
        pltpu.VMEM((TM, ATTN_WIDTH), _F32),
        pltpu.VMEM((KV_HALO + TM, KV_WIDTH), _BF16),
        pltpu.VMEM((KV_HALO + TM, KV_WIDTH), _F32),
        pltpu.VMEM((TM, _GATES_WIDTH), _F32),
        pltpu.VMEM((CONV_HALO + TM, CONV_WIDTH), _F32),
        pltpu.VMEM((POOL_HALO + TM, POOL_WIDTH), _F32),
        pltpu.VMEM((POOL_HALO + TM, POOL_WIDTH), _F32),
        pltpu.VMEM((POOL_HALO + TM, POOL_WIDTH), _F32),
        pltpu.VMEM((POOL_HALO + TM, POOL_WIDTH), _F32),
        pltpu.VMEM((TM, D_MODEL), _BF16),
        pltpu.VMEM((D_MODEL, D_FF), _BF16),
        pltpu.VMEM((D_FF, D_MODEL), _BF16),
        pltpu.SemaphoreType.DMA((2,)),
    ]
    assert len(scratch) == _N_BASE_SCRATCH
    scratch += cast_scratch
    outs = pl.pallas_call(
        functools.partial(_layer_kernel, layer=layer, final=final, cast_next=cast_next,
                          tiles_per_seq=tiles_per_seq),
        grid=(n_tiles + 1,),
        in_specs=in_specs,
        out_specs=out_specs,
        out_shape=out_shape,
        scratch_shapes=scratch,
        compiler_params=pltpu.CompilerParams(
            dimension_semantics=("arbitrary",),
            vmem_limit_bytes=V7X_VMEM_LIMIT_BYTES,
        ),
        name="hybrid_layer_final" if final else "hybrid_layer",
    )(*args)
    return outs[0], outs[1:]


def kernel(x, norm1, w_in, conv_w, sinks, pool_w, pool_scale, w_out, norm2, w1, w2, rel_bias, final_norm):
    depth = w_in.shape[0]
    bkt = jnp.asarray(_bucket_table())
    n1 = norm1.reshape(depth, 1, D_MODEL)
    n2 = norm2.reshape(depth, 1, D_MODEL)
    poolw = pool_w.reshape(depth, POOL_WIDTH, POOL_GROUP)
    pscale = pool_scale.reshape(depth, 1, POOL_WIDTH)
    bias, wbd = _tables_call(rel_bias, poolw, bkt)
    weights_f32 = (w_in, w_out, w1, w2)
    win, wout, w1b, w2b = [w[0].astype(_BF16) for w in weights_f32]
    for l in range(depth):
        last = l == depth - 1
        fnorm = final_norm.reshape(1, D_MODEL) if last else None
        x, nxt = _layer_call(x, l, n1, win, conv_w, sinks, bias, wbd, pscale, wout, n2, w1b, w2b, fnorm,
                             None if last else weights_f32)
        if not last:
            win, wout, w1b, w2b = nxt
    return x
```

```python
import functools
import math

import numpy as np
import jax
import jax.numpy as jnp
from jax import lax
from jax.experimental import pallas as pl
from jax.experimental.pallas import tpu as pltpu

D_MODEL = 1024
HEAD_DIM = 64
ATTN_WIDTH = 512
CONV_WIDTH = 256
CONV_K = 3
POOL_WIDTH = 256
N_Q_HEADS = 8
N_KV_HEADS = 2
GROUP = N_Q_HEADS // N_KV_HEADS
KV_WIDTH = N_KV_HEADS * HEAD_DIM
POOL_WINDOWS = (2, 4, 8, 16)
POOL_GROUP = 64
IN_WIDTH = 1792
D_FF = 4096
WINDOW = 128
BLOCK = 128
N_BUCKETS = 32
MAX_DISTANCE = 128
EPS = 1e-6
NEG = -1e30
LOG2E = math.log2(math.e)

_QKV_END = ATTN_WIDTH + 2 * KV_WIDTH
_GATES_WIDTH = IN_WIDTH - _QKV_END

TM = 512
KV_HALO = BLOCK
CONV_HALO = 8
POOL_HALO = 32
FF_PIECE = 512
BF16_SUBLANES = 16
V7X_VMEM_LIMIT_BYTES = 60 * 1024 * 1024
_PHASE_ORDER = "m x m x m x m x m x m x m x m x x x x x m x x m m".split()

_F32 = jnp.float32
_BF16 = jnp.bfloat16


def _bucket_table():
    kj = np.arange(2 * BLOCK, dtype=np.int32)[:, None]
    qi = np.arange(BLOCK, dtype=np.int32)[None, :] + BLOCK
    dist = qi - kj
    n = np.maximum(dist, 0)
    max_exact = N_BUCKETS // 2
    nf = np.maximum(n, 1).astype(np.float32)
    large = max_exact + (np.log(nf / np.float32(max_exact)) / np.float32(math.log(MAX_DISTANCE / max_exact))
                         * np.float32(N_BUCKETS - max_exact)).astype(np.int32)
    large = np.minimum(large, N_BUCKETS - 1)
    bucket = np.where(n < max_exact, n, large)
    valid = (dist >= 0) & (dist < WINDOW)
    return np.where(valid, bucket, -1).astype(np.int32)


def _rmsnorm(x, g):
    ms = jnp.mean(x * x, axis=-1, keepdims=True)
    return x * lax.rsqrt(ms + EPS) * g


def _tables_kernel(relb_ref, poolw_ref, bkt_ref, bias_ref, wbd_ref):
    bkt = bkt_ref[...]
    for h in range(N_Q_HEADS):
        acc = jnp.where(bkt < 0, NEG, 0.0).astype(_F32)
        for b in range(N_BUCKETS):
            acc = jnp.where(bkt == b, relb_ref[b, h] * LOG2E, acc)
        g, gi = divmod(h, GROUP)
        bias_ref[g, :, gi * BLOCK:(gi + 1) * BLOCK] = acc
    rows = lax.broadcasted_iota(jnp.int32, (POOL_GROUP, POOL_WIDTH), 0)
    cols = lax.broadcasted_iota(jnp.int32, (POOL_GROUP, POOL_WIDTH), 1)
    rep = jnp.where((cols % POOL_GROUP) == rows, 1.0, 0.0).astype(_BF16)
    r2 = lax.broadcasted_iota(jnp.int32, (POOL_WIDTH, POOL_WIDTH), 0) // POOL_GROUP
    c2 = lax.broadcasted_iota(jnp.int32, (POOL_WIDTH, POOL_WIDTH), 1) // POOL_GROUP
    for l in range(poolw_ref.shape[0]):
        tiled = jnp.dot(poolw_ref[l].astype(_BF16), rep, preferred_element_type=_F32)
        wbd_ref[l] = jnp.where(r2 == c2, tiled, 0.0).astype(_BF16)


def _tables_call(relb, poolw, bkt):
    depth = poolw.shape[0]
    vmem = pl.BlockSpec(memory_space=pltpu.VMEM)
    return pl.pallas_call(
        _tables_kernel,
        in_specs=[pl.BlockSpec(memory_space=pltpu.SMEM), vmem, vmem],
        out_specs=[vmem, vmem],
        out_shape=[jax.ShapeDtypeStruct((N_KV_HEADS, 2 * BLOCK, GROUP * BLOCK), _F32),
                   jax.ShapeDtypeStruct((depth, POOL_WIDTH, POOL_WIDTH), _BF16)],
        name="hybrid_tables",
    )(relb, poolw, bkt)


def _attention_scores(n, g, first_tile, q_scr, k_scr, bias_scr, sink_row):
    r0 = n * BLOCK
    kband = k_scr[r0:r0 + 2 * BLOCK, :]
    zeros = jnp.zeros((HEAD_DIM, GROUP * BLOCK), _F32)
    scale = LOG2E / math.sqrt(HEAD_DIM)
    qg_t = (q_scr[r0:r0 + BLOCK, g * GROUP * HEAD_DIM:(g + 1) * GROUP * HEAD_DIM] * scale).T
    qcat = jnp.concatenate([qg_t[gi * HEAD_DIM:(gi + 1) * HEAD_DIM, :] for gi in range(GROUP)], axis=1)
    rhs = jnp.concatenate([qcat, zeros] if g == 0 else [zeros, qcat], axis=0).astype(_BF16)
    s_t = jnp.dot(kband, rhs, preferred_element_type=_F32) + bias_scr[g]
    if n == 0:
        s_t = jnp.concatenate([s_t[:BLOCK] + jnp.where(first_tile, NEG, 0.0), s_t[BLOCK:]], axis=0)
    m = jnp.maximum(jnp.max(s_t, axis=0, keepdims=True), sink_row)
    return jnp.exp2(s_t - m).astype(_BF16), jnp.exp2(sink_row - m)


def _attention_output(n, g, e, e_sink, v_scr, mixed_scr):
    r0 = n * BLOCK
    v_t = v_scr[r0:r0 + 2 * BLOCK, :].T[g * HEAD_DIM:(g + 1) * HEAD_DIM, :]
    v_aug = jnp.concatenate([v_t, jnp.ones((BF16_SUBLANES, 2 * BLOCK), _F32)], axis=0).astype(_BF16)
    o_aug = jnp.dot(v_aug, e, preferred_element_type=_F32)
    inv_denom = 1.0 / (o_aug[HEAD_DIM:HEAD_DIM + 1, :] + e_sink)
    o_t = o_aug[0:HEAD_DIM, :] * inv_denom
    o4 = jnp.concatenate([o_t[:, gi * BLOCK:(gi + 1) * BLOCK] for gi in range(GROUP)], axis=0)
    mixed_scr[r0:r0 + BLOCK, g * GROUP * HEAD_DIM:(g + 1) * GROUP * HEAD_DIM] = o4.T.astype(_BF16)


def _mlp_half(x1_ref, h2_ref, w1_ref, w2_ref, fnorm_ref, o_ref, act_scr):
    for lo in range(0, D_FF, FF_PIECE):
        hid = jnp.dot(h2_ref[...], w1_ref[:, lo:lo + FF_PIECE], preferred_element_type=_F32)
        act_scr[:, lo:lo + FF_PIECE] = jnp.square(jnp.maximum(hid.astype(_BF16), 0))
        yield
    halfd = D_MODEL // 2
    parts = []
    for i in range(2):
        parts.append(jnp.dot(act_scr[...], w2_ref[:, i * halfd:(i + 1) * halfd], preferred_element_type=_F32))
        yield
    out = x1_ref[...] + jnp.concatenate(parts, axis=1)
    if fnorm_ref is not None:
        out = _rmsnorm(out, fnorm_ref[...])
    o_ref[0] = out
    yield


def _mixer_half(x_ref, x1_ref, h2_ref, seq_tile, n1_ref, win_ref, convw_ref, sinks_ref, pscale_ref, wout_ref,
                n2_ref, layer, q_scr, k_scr, v_scr, g_scr, u_scr, p_scr, a1_scr, a2_scr, a3_scr, mixed_scr,
                bias_scr, wbd_scr):
    first_tile = seq_tile == 0

    x = x_ref[0]
    h = _rmsnorm(x, n1_ref[...]).astype(_BF16)
    q_scr[...] = jnp.dot(h, win_ref[:, 0:ATTN_WIDTH], preferred_element_type=_F32)
    for r in range(0, TM, TM // 2):
        kv = jnp.dot(h[r:r + TM // 2], win_ref[:, ATTN_WIDTH:_QKV_END], preferred_element_type=_F32)
        k_scr[KV_HALO + r:KV_HALO + r + TM // 2, :] = kv[:, 0:KV_WIDTH].astype(_BF16)
        v_scr[KV_HALO + r:KV_HALO + r + TM // 2, :] = kv[:, KV_WIDTH:2 * KV_WIDTH]
    yield

    lane = lax.broadcasted_iota(jnp.int32, (1, GROUP * BLOCK), 1)
    sink_rows = []
    for g in range(N_KV_HEADS):
        row = jnp.full((1, GROUP * BLOCK), sinks_ref[layer, g * GROUP + GROUP - 1], _F32)
        for gi in range(GROUP - 2, -1, -1):
            row = jnp.where(lane < (gi + 1) * BLOCK, sinks_ref[layer, g * GROUP + gi], row)
        sink_rows.append(row * LOG2E)
    gates_half = _GATES_WIDTH // 2
    n_pairs = (TM // BLOCK) * N_KV_HEADS
    pending = None
    for k in range(n_pairs):
        n, g = divmod(k, N_KV_HEADS)
        if pending is not None:
            _attention_output(*pending, v_scr, mixed_scr)
        pending = (n, g) + _attention_scores(n, g, first_tile, q_scr, k_scr, bias_scr, sink_rows[g])
        yield
        i = k - (n_pairs - 2)
        if i >= 0:
            lo = _QKV_END + i * gates_half
            g_scr[:, i * gates_half:(i + 1) * gates_half] = jnp.dot(h, win_ref[:, lo:lo + gates_half],
                                                                    preferred_element_type=_F32)
            yield
    _attention_output(*pending, v_scr, mixed_scr)
    k_scr[0:KV_HALO, :] = k_scr[TM:TM + KV_HALO, :]
    v_scr[0:KV_HALO, :] = v_scr[TM:TM + KV_HALO, :]

    c0 = CONV_HALO
    u_scr[c0:c0 + TM, :] = g_scr[:, CONV_WIDTH:2 * CONV_WIDTH] * g_scr[:, 2 * CONV_WIDTH:3 * CONV_WIDTH]
    cw = convw_ref[...]
    y = (cw[0:1, :] * u_scr[c0 - 2:c0 - 2 + TM, :] + cw[1:2, :] * u_scr[c0 - 1:c0 - 1 + TM, :]
         + cw[2:3, :] * u_scr[c0:c0 + TM, :])
    mixed_scr[:, ATTN_WIDTH:ATTN_WIDTH + CONV_WIDTH] = (g_scr[:, 0:CONV_WIDTH] * y).astype(_BF16)
    u_scr[0:CONV_HALO, :] = u_scr[TM:TM + CONV_HALO, :]

    p0 = POOL_HALO
    pe = p0 + TM
    p = g_scr[:, 3 * CONV_WIDTH:_GATES_WIDTH]
    p_scr[p0:pe, :] = p
    a1_scr[8:pe, :] = p_scr[8:pe, :] + p_scr[7:pe - 1, :]
    a2_scr[16:pe, :] = a1_scr[16:pe, :] + a1_scr[14:pe - 2, :]
    a3_scr[24:pe, :] = a2_scr[24:pe, :] + a2_scr[20:pe - 4, :]
    s16 = a3_scr[p0:pe, :] + a3_scr[p0 - 8:pe - 8, :]
    plane = lax.broadcasted_iota(jnp.int32, (TM, POOL_WIDTH), 1)
    ssum = jnp.where(plane < POOL_GROUP, a1_scr[p0:pe, :],
                     jnp.where(plane < 2 * POOL_GROUP, a2_scr[p0:pe, :],
                               jnp.where(plane < 3 * POOL_GROUP, a3_scr[p0:pe, :], s16)))
    win = jnp.where(plane < POOL_GROUP, POOL_WINDOWS[0],
                    jnp.where(plane < 2 * POOL_GROUP, POOL_WINDOWS[1],
                              jnp.where(plane < 3 * POOL_GROUP, POOL_WINDOWS[2], POOL_WINDOWS[3])))
    t = seq_tile * TM + lax.broadcasted_iota(jnp.int32, (TM, POOL_WIDTH), 0)
    count = jnp.minimum(t + 1, win).astype(_F32)
    pooled = (ssum / count - p).astype(_BF16)
    p_scr[0:POOL_HALO, :] = p_scr[TM:TM + POOL_HALO, :]
    yield
    for r in range(0, TM, TM // 2):
        mixed_pool = jnp.dot(pooled[r:r + TM // 2], wbd_scr[...], preferred_element_type=_F32) * pscale_ref[...]
        mixed_scr[r:r + TM // 2, ATTN_WIDTH + CONV_WIDTH:D_MODEL] = mixed_pool.astype(_BF16)
    yield

    x1 = x + jnp.dot(mixed_scr[...], wout_ref[...], preferred_element_type=_F32)
    x1_ref[...] = x1
    h2_ref[...] = _rmsnorm(x1, n2_ref[...]).astype(_BF16)
    yield


def _step_rows(s, buf):
    rows = buf.shape[0]
    return pl.ds(pl.multiple_of(s * rows, rows), rows)


def _fetch_copies(s, layer, srcs, in_bufs, sems):
    return [pltpu.make_async_copy(src.at[layer + 1, _step_rows(s, buf), :], buf, sems.at[i])
            for i, (src, buf) in enumerate(zip(srcs, in_bufs))]


def _put_copies(s, dsts, out_bufs, sems):
    return [pltpu.make_async_copy(buf, dst.at[_step_rows(s, buf), :], sems.at[i])
            for i, (dst, buf) in enumerate(zip(dsts, out_bufs))]


_N_BASE_SCRATCH = 16


def _layer_kernel(x_ref, n1_ref, win_ref, convw_ref, sinks_ref, bias_scr, wbd_scr, pscale_ref, wout_ref,
                  n2_ref, w1_hbm, w2_hbm, *rest, layer, final, cast_next, tiles_per_seq):
    rest = list(rest)
    fnorm_ref = rest.pop(0) if final else None
    next_f32 = [rest.pop(0) for _ in range(4)] if cast_next else []
    o_ref = rest.pop(0)
    next_bf16 = [rest.pop(0) for _ in range(4)] if cast_next else []
    (x1_scr, h2_scr, act_scr, q_scr, k_scr, v_scr, g_scr, u_scr, p_scr, a1_scr, a2_scr, a3_scr, mixed_scr,
     w1_ref, w2_ref, mlp_w_sems) = rest[:_N_BASE_SCRATCH]
    s = pl.program_id(0)
    n_tiles = pl.num_programs(0) - 1
    slot = s % 2
    seq_tile = jnp.minimum(s, n_tiles - 1) % tiles_per_seq

    mlp_w_copies = [pltpu.make_async_copy(src, dst, mlp_w_sems.at[i])
                    for i, (src, dst) in enumerate(((w1_hbm, w1_ref), (w2_hbm, w2_ref)))]

    @pl.when(s == 0)
    def _():
        for copy in mlp_w_copies:
            copy.start()

    @pl.when(s == 1)
    def _():
        for copy in mlp_w_copies:
            copy.wait()

    if cast_next:
        cast_scratch = rest[_N_BASE_SCRATCH:]
        in_bufs, out_bufs, (in_sems, out_sems) = cast_scratch[0:4], cast_scratch[4:8], cast_scratch[8:10]

        @pl.when(s < n_tiles)
        def _():
            for fetch in _fetch_copies(s, layer, next_f32, in_bufs, in_sems):
                fetch.start()

    @pl.when(jnp.logical_and(seq_tile == 0, s < n_tiles))
    def _():
        k_scr[0:KV_HALO, :] = jnp.zeros((KV_HALO, KV_WIDTH), _BF16)
        v_scr[0:KV_HALO, :] = jnp.zeros((KV_HALO, KV_WIDTH), _F32)
        u_scr[0:CONV_HALO, :] = jnp.zeros((CONV_HALO, CONV_WIDTH), _F32)
        p_scr[0:POOL_HALO, :] = jnp.zeros((POOL_HALO, POOL_WIDTH), _F32)

    def mlp_half():
        return _mlp_half(x1_scr.at[1 - slot], h2_scr.at[1 - slot], w1_ref, w2_ref, fnorm_ref if final else None,
                         o_ref, act_scr)

    def mixer_half():
        return _mixer_half(x_ref, x1_scr.at[slot], h2_scr.at[slot], seq_tile, n1_ref, win_ref, convw_ref,
                           sinks_ref, pscale_ref, wout_ref, n2_ref, layer, q_scr, k_scr, v_scr, g_scr, u_scr, p_scr,
                           a1_scr, a2_scr, a3_scr, mixed_scr, bias_scr, wbd_scr)

    @pl.when(s == 0)
    def _():
        for _ in mixer_half():
            pass

    @pl.when(jnp.logical_and(s > 0, s < n_tiles))
    def _():
        mlp, mixer = mlp_half(), mixer_half()
        for who in _PHASE_ORDER:
            next(mlp if who == "m" else mixer)
        for half in (mlp, mixer):
            assert next(half, "done") == "done", "phase order does not cover every phase"

    @pl.when(s == n_tiles)
    def _():
        for _ in mlp_half():
            pass

    if cast_next:
        @pl.when(s < n_tiles)
        def _():
            @pl.when(s > 0)
            def _():
                for put in _put_copies(s - 1, next_bf16, out_bufs, out_sems):
                    put.wait()

            for fetch, put, ibuf, obuf in zip(_fetch_copies(s, layer, next_f32, in_bufs, in_sems),
                                              _put_copies(s, next_bf16, out_bufs, out_sems), in_bufs, out_bufs):
                fetch.wait()
                obuf[...] = ibuf[...].astype(_BF16)
                put.start()

        @pl.when(s == n_tiles)
        def _():
            for put in _put_copies(s - 1, next_bf16, out_bufs, out_sems):
                put.wait()


def _resident(shape, layer=None):
    if layer is None:
        return pl.BlockSpec(shape, lambda s: (0,) * len(shape), pipeline_mode=pl.Buffered(1))
    return pl.BlockSpec((None,) + shape, lambda s: (layer,) + (0,) * len(shape), pipeline_mode=pl.Buffered(1))


def _layer_call(x, layer, n1, win, convw, sinks, bias, wbd, pscale, wout, n2, w1, w2, fnorm, next_f32):
    batch, seq, _ = x.shape
    assert seq % TM == 0 and TM % BLOCK == 0
    tiles_per_seq = seq // TM
    n_tiles = batch * tiles_per_seq
    final = fnorm is not None
    cast_next = next_f32 is not None
    smem = pl.BlockSpec(memory_space=pltpu.SMEM)
    hbm = pl.BlockSpec(memory_space=pl.ANY)

    def in_tile(s):
        t = jnp.minimum(s, n_tiles - 1)
        return (t // tiles_per_seq, t % tiles_per_seq, 0)

    def out_tile(s):
        t = jnp.maximum(s - 1, 0)
        return (t // tiles_per_seq, t % tiles_per_seq, 0)

    in_specs = [
        pl.BlockSpec((1, TM, D_MODEL), in_tile),
        _resident((1, D_MODEL), layer),
        _resident((D_MODEL, IN_WIDTH)),
        _resident((CONV_K, CONV_WIDTH), layer),
        smem,
        _resident((N_KV_HEADS, 2 * BLOCK, GROUP * BLOCK)),
        _resident((POOL_WIDTH, POOL_WIDTH), layer),
        _resident((1, POOL_WIDTH), layer),
        _resident((D_MODEL, D_MODEL)),
        _resident((1, D_MODEL), layer),
        hbm,
        hbm,
    ]
    args = [x, n1, win, convw, sinks, bias, wbd, pscale, wout, n2, w1, w2]
    if final:
        in_specs.append(_resident((1, D_MODEL)))
        args.append(fnorm)
    out_specs = [pl.BlockSpec((1, TM, D_MODEL), out_tile)]
    out_shape = [jax.ShapeDtypeStruct(x.shape, x.dtype)]
    cast_scratch = []
    if cast_next:
        in_specs += [hbm] * len(next_f32)
        args += list(next_f32)
        out_specs += [hbm] * len(next_f32)
        out_shape += [jax.ShapeDtypeStruct(w.shape[1:], _BF16) for w in next_f32]
        rows = [w.shape[1] // n_tiles for w in next_f32]
        assert all(r * n_tiles == w.shape[1] and r % BF16_SUBLANES == 0 for r, w in zip(rows, next_f32))
        cast_scratch = ([pltpu.VMEM((r, w.shape[2]), _F32) for r, w in zip(rows, next_f32)]
                        + [pltpu.VMEM((r, w.shape[2]), _BF16) for r, w in zip(rows, next_f32)]
                        + [pltpu.SemaphoreType.DMA((len(next_f32),))] * 2)
    scratch = [
        pltpu.VMEM((2, TM, D_MODEL), _F32),
        pltpu.VMEM((2, TM, D_MODEL), _BF16),
        pltpu.VMEM((TM, D_FF), _BF16),
        pltpu.VMEM((TM, ATTN_WIDTH), _F32),
        pltpu.VMEM((KV_HALO + TM, KV_WIDTH), _BF16),
        pltpu.VMEM((KV_HALO + TM, KV_WIDTH), _F32),
        pltpu.VMEM((TM, _GATES_WIDTH), _F32),
        pltpu.VMEM((CONV_HALO + TM, CONV_WIDTH), _F32),
        pltpu.VMEM((POOL_HALO + TM, POOL_WIDTH), _F32),
        pltpu.VMEM((POOL_HALO + TM, POOL_WIDTH), _F32),
        pltpu.VMEM((POOL_HALO + TM, POOL_WIDTH), _F32),
        pltpu.VMEM((POOL_HALO + TM, POOL_WIDTH), _F32),
        pltpu.VMEM((TM, D_MODEL), _BF16),
        pltpu.VMEM((D_MODEL, D_FF), _BF16),
        pltpu.VMEM((D_FF, D_MODEL), _BF16),
        pltpu.SemaphoreType.DMA((2,)),
    ]
    assert len(scratch) == _N_BASE_SCRATCH
    scratch += cast_scratch
    outs = pl.pallas_call(
        functools.partial(_layer_kernel, layer=layer, final=final, cast_next=cast_next,
                          tiles_per_seq=tiles_per_seq),
        grid=(n_tiles + 1,),
        in_specs=in_specs,
        out_specs=out_specs,
        out_shape=out_shape,
        scratch_shapes=scratch,
        compiler_params=pltpu.CompilerParams(
            dimension_semantics=("arbitrary",),
            vmem_limit_bytes=V7X_VMEM_LIMIT_BYTES,
        ),
        name="hybrid_layer_final" if final else "hybrid_layer",
    )(*args)
    return outs[0], outs[1:]


def kernel(x, norm1, w_in, conv_w, sinks, pool_w, pool_scale, w_out, norm2, w1, w2, rel_bias, final_norm):
    depth = w_in.shape[0]
    bkt = jnp.asarray(_bucket_table())
    n1 = norm1.reshape(depth, 1, D_MODEL)
    n2 = norm2.reshape(depth, 1, D_MODEL)
    poolw = pool_w.reshape(depth, POOL_WIDTH, POOL_GROUP)
    pscale = pool_scale.reshape(depth, 1, POOL_WIDTH)
    bias, wbd = _tables_call(rel_bias, poolw, bkt)
    weights_f32 = (w_in, w_out, w1, w2)
    win, wout, w1b, w2b = [w[0].astype(_BF16) for w in weights_f32]
    for l in range(depth):
        last = l == depth - 1
        fnorm = final_norm.reshape(1, D_MODEL) if last else None
        x, nxt = _layer_call(x, l, n1, win, conv_w, sinks, bias, wbd, pscale, wout, n2, w1b, w2b, fnorm,
                             None if last else weights_f32)
        if not last:
            win, wout, w1b, w2b = nxt
    return x
```

```python
import functools
import math

import numpy as np
import jax
import jax.numpy as jnp
from jax import lax
from jax.experimental import pallas as pl
from jax.experimental.pallas import tpu as pltpu

D_MODEL = 1024
HEAD_DIM = 64
ATTN_WIDTH = 512
CONV_WIDTH = 256
CONV_K = 3
POOL_WIDTH = 256
N_Q_HEADS = 8
N_KV_HEADS = 2
GROUP = N_Q_HEADS // N_KV_HEADS
KV_WIDTH = N_KV_HEADS * HEAD_DIM
POOL_WINDOWS = (2, 4, 8, 16)
POOL_GROUP = 64
IN_WIDTH = 1792
D_FF = 4096
WINDOW = 128
BLOCK = 128
N_BUCKETS = 32
MAX_DISTANCE = 128
EPS = 1e-6
NEG = -1e30
LOG2E = math.log2(math.e)

_QKV_END = ATTN_WIDTH + 2 * KV_WIDTH
_GATES_WIDTH = IN_WIDTH - _QKV_END

TM = 512
KV_HALO = BLOCK
CONV_HALO = 8
POOL_HALO = 32
FF_PIECE = 512
BF16_SUBLANES = 16
V7X_VMEM_LIMIT_BYTES = 60 * 1024 * 1024
_PHASE_ORDER = "m x m x m x m x m x m x m x m x x x x x m x x m m".split()

_F32 = jnp.float32
_BF16 = jnp.bfloat16


def _bucket_table():
    kj = np.arange(2 * BLOCK, dtype=np.int32)[:, None]
    qi = np.arange(BLOCK, dtype=np.int32)[None, :] + BLOCK
    dist = qi - kj
    n = np.maximum(dist, 0)
    max_exact = N_BUCKETS // 2
    nf = np.maximum(n, 1).astype(np.float32)
    large = max_exact + (np.log(nf / np.float32(max_exact)) / np.float32(math.log(MAX_DISTANCE / max_exact))
                         * np.float32(N_BUCKETS - max_exact)).astype(np.int32)
    large = np.minimum(large, N_BUCKETS - 1)
    bucket = np.where(n < max_exact, n, large)
    valid = (dist >= 0) & (dist < WINDOW)
    return np.where(valid, bucket, -1).astype(np.int32)


def _rmsnorm(x, g):
    ms = jnp.mean(x * x, axis=-1, keepdims=True)
    return x * lax.rsqrt(ms + EPS) * g


def _tables_kernel(relb_ref, poolw_ref, bkt_ref, bias_ref, wbd_ref):
    bkt = bkt_ref[...]
    for h in range(N_Q_HEADS):
        acc = jnp.where(bkt < 0, NEG, 0.0).astype(_F32)
        for b in range(N_BUCKETS):
            acc = jnp.where(bkt == b, relb_ref[b, h] * LOG2E, acc)
        g, gi = divmod(h, GROUP)
        bias_ref[g, :, gi * BLOCK:(gi + 1) * BLOCK] = acc
    rows = lax.broadcasted_iota(jnp.int32, (POOL_GROUP, POOL_WIDTH), 0)
    cols = lax.broadcasted_iota(jnp.int32, (POOL_GROUP, POOL_WIDTH), 1)
    rep = jnp.where((cols % POOL_GROUP) == rows, 1.0, 0.0).astype(_BF16)
    r2 = lax.broadcasted_iota(jnp.int32, (POOL_WIDTH, POOL_WIDTH), 0) // POOL_GROUP
    c2 = lax.broadcasted_iota(jnp.int32, (POOL_WIDTH, POOL_WIDTH), 1) // POOL_GROUP
    for l in range(poolw_ref.shape[0]):
        tiled = jnp.dot(poolw_ref[l].astype(_BF16), rep, preferred_element_type=_F32)
        wbd_ref[l] = jnp.where(r2 == c2, tiled, 0.0).astype(_BF16)


def _tables_call(relb, poolw, bkt):
    depth = poolw.shape[0]
    vmem = pl.BlockSpec(memory_space=pltpu.VMEM)
    return pl.pallas_call(
        _tables_kernel,
        in_specs=[pl.BlockSpec(memory_space=pltpu.SMEM), vmem, vmem],
        out_specs=[vmem, vmem],
        out_shape=[jax.ShapeDtypeStruct((N_KV_HEADS, 2 * BLOCK, GROUP * BLOCK), _F32),
                   jax.ShapeDtypeStruct((depth, POOL_WIDTH, POOL_WIDTH), _BF16)],
        name="hybrid_tables",
    )(relb, poolw, bkt)


def _attention_scores(n, g, first_tile, q_scr, k_scr, bias_scr, sink_row):
    r0 = n * BLOCK
    kband = k_scr[r0:r0 + 2 * BLOCK, :]
    zeros = jnp.zeros((HEAD_DIM, GROUP * BLOCK), _F32)
    scale = LOG2E / math.sqrt(HEAD_DIM)
    qg_t = (q_scr[r0:r0 + BLOCK, g * GROUP * HEAD_DIM:(g + 1) * GROUP * HEAD_DIM] * scale).T
    qcat = jnp.concatenate([qg_t[gi * HEAD_DIM:(gi + 1) * HEAD_DIM, :] for gi in range(GROUP)], axis=1)
    rhs = jnp.concatenate([qcat, zeros] if g == 0 else [zeros, qcat], axis=0).astype(_BF16)
    s_t = jnp.dot(kband, rhs, preferred_element_type=_F32) + bias_scr[g]
    if n == 0:
        s_t = jnp.concatenate([s_t[:BLOCK] + jnp.where(first_tile, NEG, 0.0), s_t[BLOCK:]], axis=0)
    m = jnp.maximum(jnp.max(s_t, axis=0, keepdims=True), sink_row)
    return jnp.exp2(s_t - m).astype(_BF16), jnp.exp2(sink_row - m)


def _attention_output(n, g, e, e_sink, v_scr, mixed_scr):
    r0 = n * BLOCK
    v_t = v_scr[r0:r0 + 2 * BLOCK, :].T[g * HEAD_DIM:(g + 1) * HEAD_DIM, :]
    v_aug = jnp.concatenate([v_t, jnp.ones((BF16_SUBLANES, 2 * BLOCK), _F32)], axis=0).astype(_BF16)
    o_aug = jnp.dot(v_aug, e, preferred_element_type=_F32)
    inv_denom = 1.0 / (o_aug[HEAD_DIM:HEAD_DIM + 1, :] + e_sink)
    o_t = o_aug[0:HEAD_DIM, :] * inv_denom
    o4 = jnp.concatenate([o_t[:, gi * BLOCK:(gi + 1) * BLOCK] for gi in range(GROUP)], axis=0)
    mixed_scr[r0:r0 + BLOCK, g * GROUP * HEAD_DIM:(g + 1) * GROUP * HEAD_DIM] = o4.T.astype(_BF16)


def _mlp_half(x1_ref, h2_ref, w1_ref, w2_ref, fnorm_ref, o_ref, act_scr):
    for lo in range(0, D_FF, FF_PIECE):
        hid = jnp.dot(h2_ref[...], w1_ref[:, lo:lo + FF_PIECE], preferred_element_type=_F32)
        act_scr[:, lo:lo + FF_PIECE] = jnp.square(jnp.maximum(hid, 0.0)).astype(_BF16)
        yield
    halfd = D_MODEL // 2
    parts = []
    for i in range(2):
        parts.append(jnp.dot(act_scr[...], w2_ref[:, i * halfd:(i + 1) * halfd], preferred_element_type=_F32))
        yield
    out = x1_ref[...] + jnp.concatenate(parts, axis=1)
    if fnorm_ref is not None:
        out = _rmsnorm(out, fnorm_ref[...])
    o_ref[0] = out
    yield


def _mixer_half(x_ref, x1_ref, h2_ref, seq_tile, n1_ref, win_ref, convw_ref, sinks_ref, pscale_ref, wout_ref,
                n2_ref, layer, q_scr, k_scr, v_scr, g_scr, u_scr, p_scr, a1_scr, a2_scr, a3_scr, mixed_scr,
                bias_scr, wbd_scr):
    first_tile = seq_tile == 0

    x = x_ref[0]
    h = _rmsnorm(x, n1_ref[...]).astype(_BF16)
    q_scr[...] = jnp.dot(h, win_ref[:, 0:ATTN_WIDTH], preferred_element_type=_F32)
    for r in range(0, TM, TM // 2):
        kv = jnp.dot(h[r:r + TM // 2], win_ref[:, ATTN_WIDTH:_QKV_END], preferred_element_type=_F32)
        k_scr[KV_HALO + r:KV_HALO + r + TM // 2, :] = kv[:, 0:KV_WIDTH].astype(_BF16)
        v_scr[KV_HALO + r:KV_HALO + r + TM // 2, :] = kv[:, KV_WIDTH:2 * KV_WIDTH]
    yield

    lane = lax.broadcasted_iota(jnp.int32, (1, GROUP * BLOCK), 1)
    sink_rows = []
    for g in range(N_KV_HEADS):
        row = jnp.full((1, GROUP * BLOCK), sinks_ref[layer, g * GROUP + GROUP - 1], _F32)
        for gi in range(GROUP - 2, -1, -1):
            row = jnp.where(lane < (gi + 1) * BLOCK, sinks_ref[layer, g * GROUP + gi], row)
        sink_rows.append(row * LOG2E)
    gates_half = _GATES_WIDTH // 2
    n_pairs = (TM // BLOCK) * N_KV_HEADS
    pending = None
    for k in range(n_pairs):
        n, g = divmod(k, N_KV_HEADS)
        if pending is not None:
            _attention_output(*pending, v_scr, mixed_scr)
        pending = (n, g) + _attention_scores(n, g, first_tile, q_scr, k_scr, bias_scr, sink_rows[g])
        yield
        i = k - (n_pairs - 2)
        if i >= 0:
            lo = _QKV_END + i * gates_half
            g_scr[:, i * gates_half:(i + 1) * gates_half] = jnp.dot(h, win_ref[:, lo:lo + gates_half],
                                                                    preferred_element_type=_F32)
            yield
    _attention_output(*pending, v_scr, mixed_scr)
    k_scr[0:KV_HALO, :] = k_scr[TM:TM + KV_HALO, :]
    v_scr[0:KV_HALO, :] = v_scr[TM:TM + KV_HALO, :]

    c0 = CONV_HALO
    u_scr[c0:c0 + TM, :] = g_scr[:, CONV_WIDTH:2 * CONV_WIDTH] * g_scr[:, 2 * CONV_WIDTH:3 * CONV_WIDTH]
    cw = convw_ref[...]
    y = (cw[0:1, :] * u_scr[c0 - 2:c0 - 2 + TM, :] + cw[1:2, :] * u_scr[c0 - 1:c0 - 1 + TM, :]
         + cw[2:3, :] * u_scr[c0:c0 + TM, :])
    mixed_scr[:, ATTN_WIDTH:ATTN_WIDTH + CONV_WIDTH] = (g_scr[:, 0:CONV_WIDTH] * y).astype(_BF16)
    u_scr[0:CONV_HALO, :] = u_scr[TM:TM + CONV_HALO, :]

    p0 = POOL_HALO
    pe = p0 + TM
    p = g_scr[:, 3 * CONV_WIDTH:_GATES_WIDTH]
    p_scr[p0:pe, :] = p
    a1_scr[8:pe, :] = p_scr[8:pe, :] + p_scr[7:pe - 1, :]
    a2_scr[16:pe, :] = a1_scr[16:pe, :] + a1_scr[14:pe - 2, :]
    a3_scr[24:pe, :] = a2_scr[24:pe, :] + a2_scr[20:pe - 4, :]
    s16 = a3_scr[p0:pe, :] + a3_scr[p0 - 8:pe - 8, :]
    plane = lax.broadcasted_iota(jnp.int32, (TM, POOL_WIDTH), 1)
    ssum = jnp.where(plane < POOL_GROUP, a1_scr[p0:pe, :],
                     jnp.where(plane < 2 * POOL_GROUP, a2_scr[p0:pe, :],
                               jnp.where(plane < 3 * POOL_GROUP, a3_scr[p0:pe, :], s16)))
    win = jnp.where(plane < POOL_GROUP, POOL_WINDOWS[0],
                    jnp.where(plane < 2 * POOL_GROUP, POOL_WINDOWS[1],
                              jnp.where(plane < 3 * POOL_GROUP, POOL_WINDOWS[2], POOL_WINDOWS[3])))
    t = seq_tile * TM + lax.broadcasted_iota(jnp.int32, (TM, POOL_WIDTH), 0)
    count = jnp.minimum(t + 1, win).astype(_F32)
    pooled = (ssum / count - p).astype(_BF16)
    p_scr[0:POOL_HALO, :] = p_scr[TM:TM + POOL_HALO, :]
    yield
    for r in range(0, TM, TM // 2):
        mixed_pool = jnp.dot(pooled[r:r + TM // 2], wbd_scr[...], preferred_element_type=_F32) * pscale_ref[...]
        mixed_scr[r:r + TM // 2, ATTN_WIDTH + CONV_WIDTH:D_MODEL] = mixed_pool.astype(_BF16)
    yield

    x1 = x + jnp.dot(mixed_scr[...], wout_ref[...], preferred_element_type=_F32)
    x1_ref[...] = x1
    h2_ref[...] = _rmsnorm(x1, n2_ref[...]).astype(_BF16)
    yield


def _step_rows(s, buf):
    rows = buf.shape[0]
    return pl.ds(pl.multiple_of(s * rows, rows), rows)


def _fetch_copies(s, layer, srcs, in_bufs, sems):
    return [pltpu.make_async_copy(src.at[layer + 1, _step_rows(s, buf), :], buf, sems.at[i])
            for i, (src, buf) in enumerate(zip(srcs, in_bufs))]


def _put_copies(s, dsts, out_bufs, sems):
    return [pltpu.make_async_copy(buf, dst.at[_step_rows(s, buf), :], sems.at[i])
            for i, (dst, buf) in enumerate(zip(dsts, out_bufs))]


_N_BASE_SCRATCH = 16


def _layer_kernel(x_ref, n1_ref, win_ref, convw_ref, sinks_ref, bias_scr, wbd_scr, pscale_ref, wout_ref,
                  n2_ref, w1_hbm, w2_hbm, *rest, layer, final, cast_next, tiles_per_seq):
    rest = list(rest)
    fnorm_ref = rest.pop(0) if final else None
    next_f32 = [rest.pop(0) for _ in range(4)] if cast_next else []
    o_ref = rest.pop(0)
    next_bf16 = [rest.pop(0) for _ in range(4)] if cast_next else []
    (x1_scr, h2_scr, act_scr, q_scr, k_scr, v_scr, g_scr, u_scr, p_scr, a1_scr, a2_scr, a3_scr, mixed_scr,
     w1_ref, w2_ref, mlp_w_sems) = rest[:_N_BASE_SCRATCH]
    s = pl.program_id(0)
    n_tiles = pl.num_programs(0) - 1
    slot = s % 2
    seq_tile = jnp.minimum(s, n_tiles - 1) % tiles_per_seq

    mlp_w_copies = [pltpu.make_async_copy(src, dst, mlp_w_sems.at[i])
                    for i, (src, dst) in enumerate(((w1_hbm, w1_ref), (w2_hbm, w2_ref)))]

    @pl.when(s == 0)
    def _():
        for copy in mlp_w_copies:
            copy.start()

    @pl.when(s == 1)
    def _():
        for copy in mlp_w_copies:
            copy.wait()

    def convert_chunk():
        pass

    def start_chunk_copies():
        pass

    if cast_next:
        cast_scratch = rest[_N_BASE_SCRATCH:]
        in_bufs, out_bufs, (in_sems, out_sems) = cast_scratch[0:4], cast_scratch[4:8], cast_scratch[8:10]

        @pl.when(s == 0)
        def _():
            for fetch in _fetch_copies(0, layer, next_f32, in_bufs, in_sems):
                fetch.start()

        @pl.when(s > 0)
        def _():
            for put in _put_copies(s - 1, next_bf16, out_bufs, out_sems):
                put.wait()

        @pl.when(s < n_tiles)
        def _():
            for fetch in _fetch_copies(s, layer, next_f32, in_bufs, in_sems):
                fetch.wait()

        def convert_chunk():
            for ibuf, obuf in zip(in_bufs, out_bufs):
                obuf[...] = ibuf[...].astype(_BF16)

        def start_chunk_copies():
            for put in _put_copies(s, next_bf16, out_bufs, out_sems):
                put.start()

            @pl.when(s + 1 < n_tiles)
            def _():
                for fetch in _fetch_copies(s + 1, layer, next_f32, in_bufs, in_sems):
                    fetch.start()

    @pl.when(jnp.logical_and(seq_tile == 0, s < n_tiles))
    def _():
        k_scr[0:KV_HALO, :] = jnp.zeros((KV_HALO, KV_WIDTH), _BF16)
        v_scr[0:KV_HALO, :] = jnp.zeros((KV_HALO, KV_WIDTH), _F32)
        u_scr[0:CONV_HALO, :] = jnp.zeros((CONV_HALO, CONV_WIDTH), _F32)
        p_scr[0:POOL_HALO, :] = jnp.zeros((POOL_HALO, POOL_WIDTH), _F32)

    def mlp_half():
        return _mlp_half(x1_scr.at[1 - slot], h2_scr.at[1 - slot], w1_ref, w2_ref, fnorm_ref if final else None,
                         o_ref, act_scr)

    def mixer_half():
        return _mixer_half(x_ref, x1_scr.at[slot], h2_scr.at[slot], seq_tile, n1_ref, win_ref, convw_ref,
                           sinks_ref, pscale_ref, wout_ref, n2_ref, layer, q_scr, k_scr, v_scr, g_scr, u_scr, p_scr,
                           a1_scr, a2_scr, a3_scr, mixed_scr, bias_scr, wbd_scr)

    @pl.when(s == 0)
    def _():
        convert_chunk()
        for _ in mixer_half():
            pass
        start_chunk_copies()

    @pl.when(jnp.logical_and(s > 0, s < n_tiles))
    def _():
        convert_chunk()
        mlp, mixer = mlp_half(), mixer_half()
        for who in _PHASE_ORDER:
            next(mlp if who == "m" else mixer)
        for half in (mlp, mixer):
            assert next(half, "done") == "done", "phase order does not cover every phase"
        start_chunk_copies()

    @pl.when(s == n_tiles)
    def _():
        for _ in mlp_half():
            pass


def _resident(shape, layer=None):
    if layer is None:
        return pl.BlockSpec(shape, lambda s: (0,) * len(shape), pipeline_mode=pl.Buffered(1))
    return pl.BlockSpec((None,) + shape, lambda s: (layer,) + (0,) * len(shape), pipeline_mode=pl.Buffered(1))


def _layer_call(x, layer, n1, win, convw, sinks, bias, wbd, pscale, wout, n2, w1, w2, fnorm, next_f32):
    batch, seq, _ = x.shape
    assert seq % TM == 0 and TM % BLOCK == 0
    tiles_per_seq = seq // TM
    n_tiles = batch * tiles_per_seq
    final = fnorm is not None
    cast_next = next_f32 is not None
    smem = pl.BlockSpec(memory_space=pltpu.SMEM)
    hbm = pl.BlockSpec(memory_space=pl.ANY)

    def in_tile(s):
        t = jnp.minimum(s, n_tiles - 1)
        return (t // tiles_per_seq, t % tiles_per_seq, 0)

    def out_tile(s):
        t = jnp.maximum(s - 1, 0)
        return (t // tiles_per_seq, t % tiles_per_seq, 0)

    in_specs = [
        pl.BlockSpec((1, TM, D_MODEL), in_tile),
        _resident((1, D_MODEL), layer),
        _resident((D_MODEL, IN_WIDTH)),
        _resident((CONV_K, CONV_WIDTH), layer),
        smem,
        _resident((N_KV_HEADS, 2 * BLOCK, GROUP * BLOCK)),
        _resident((POOL_WIDTH, POOL_WIDTH), layer),
        _resident((1, POOL_WIDTH), layer),
        _resident((D_MODEL, D_MODEL)),
        _resident((1, D_MODEL), layer),
        hbm,
        hbm,
    ]
    args = [x, n1, win, convw, sinks, bias, wbd, pscale, wout, n2, w1, w2]
    if final:
        in_specs.append(_resident((1, D_MODEL)))
        args.append(fnorm)
    out_specs = [pl.BlockSpec((1, TM, D_MODEL), out_tile)]
    out_shape = [jax.ShapeDtypeStruct(x.shape, x.dtype)]
    cast_scratch = []
    if cast_next:
        in_specs += [hbm] * len(next_f32)
        args += list(next_f32)
        out_specs += [hbm] * len(next_f32)
        out_shape += [jax.ShapeDtypeStruct(w.shape[1:], _BF16) for w in next_f32]
        rows = [w.shape[1] // n_tiles for w in next_f32]
        assert all(r * n_tiles == w.shape[1] and r % BF16_SUBLANES == 0 for r, w in zip(rows, next_f32))
        cast_scratch = ([pltpu.VMEM((r, w.shape[2]), _F32) for r, w in zip(rows, next_f32)]
                        + [pltpu.VMEM((r, w.shape[2]), _BF16) for r, w in zip(rows, next_f32)]
                        + [pltpu.SemaphoreType.DMA((len(next_f32),))] * 2)
    scratch = [
        pltpu.VMEM((2, TM, D_MODEL), _F32),
        pltpu.VMEM((2, TM, D_MODEL), _BF16),
        pltpu.VMEM((TM, D_FF), _BF16),
        pltpu.VMEM((TM, ATTN_WIDTH), _F32),
        pltpu.VMEM((KV_HALO + TM, KV_WIDTH), _BF16),
        pltpu.VMEM((KV_HALO + TM, KV_WIDTH), _F32),
        pltpu.VMEM((TM, _GATES_WIDTH), _F32),
        pltpu.VMEM((CONV_HALO + TM, CONV_WIDTH), _F32),
        pltpu.VMEM((POOL_HALO + TM, POOL_WIDTH), _F32),
        pltpu.VMEM((POOL_HALO + TM, POOL_WIDTH), _F32),
        pltpu.VMEM((POOL_HALO + TM, POOL_WIDTH), _F32),
        pltpu.VMEM((POOL_HALO + TM, POOL_WIDTH), _F32),
        pltpu.VMEM((TM, D_MODEL), _BF16),
        pltpu.VMEM((D_MODEL, D_FF), _BF16),
        pltpu.VMEM((D_FF, D_MODEL), _BF16),
        pltpu.SemaphoreType.DMA((2,)),
    ]
    assert len(scratch) == _N_BASE_SCRATCH
    scratch += cast_scratch
    outs = pl.pallas_call(
        functools.partial(_layer_kernel, layer=layer, final=final, cast_next=cast_next,
                          tiles_per_seq=tiles_per_seq),
        grid=(n_tiles + 1,),
        in_specs=in_specs,
        out_specs=out_specs,
        out_shape=out_shape,
        scratch_shapes=scratch,
        compiler_params=pltpu.CompilerParams(
            dimension_semantics=("arbitrary",),
            vmem_limit_bytes=V7X_VMEM_LIMIT_BYTES,
        ),
        name="hybrid_layer_final" if final else "hybrid_layer",
    )(*args)
    return outs[0], outs[1:]


def kernel(x, norm1, w_in, conv_w, sinks, pool_w, pool_scale, w_out, norm2, w1, w2, rel_bias, final_norm):
    depth = w_in.shape[0]
    bkt = jnp.asarray(_bucket_table())
    n1 = norm1.reshape(depth, 1, D_MODEL)
    n2 = norm2.reshape(depth, 1, D_MODEL)
    poolw = pool_w.reshape(depth, POOL_WIDTH, POOL_GROUP)
    pscale = pool_scale.reshape(depth, 1, POOL_WIDTH)
    bias, wbd = _tables_call(rel_bias, poolw, bkt)
    weights_f32 = (w_in, w_out, w1, w2)
    win, wout, w1b, w2b = [w[0].astype(_BF16) for w in weights_f32]
    for l in range(depth):
        last = l == depth - 1
        fnorm = final_norm.reshape(1, D_MODEL) if last else None
        x, nxt = _layer_call(x, l, n1, win, conv_w, sinks, bias, wbd, pscale, wout, n2, w1b, w2b, fnorm,
                             None if last else weights_f32)
        if not last:
            win, wout, w1b, w2b = nxt
    return x
```

```python
import functools
import math

import numpy as np
import jax
import jax.numpy as jnp
from jax import lax
from jax.experimental import pallas as pl
from jax.experimental.pallas import tpu as pltpu

D_MODEL = 1024
HEAD_DIM = 64
ATTN_WIDTH = 512
CONV_WIDTH = 256
CONV_K = 3
POOL_WIDTH = 256
N_Q_HEADS = 8
N_KV_HEADS = 2
GROUP = N_Q_HEADS // N_KV_HEADS
KV_WIDTH = N_KV_HEADS * HEAD_DIM
POOL_WINDOWS = (2, 4, 8, 16)
POOL_GROUP = 64
IN_WIDTH = 1792
D_FF = 4096
WINDOW = 128
BLOCK = 128
N_BUCKETS = 32
MAX_DISTANCE = 128
EPS = 1e-6
NEG = -1e30
LOG2E = math.log2(math.e)

_QKV_END = ATTN_WIDTH + 2 * KV_WIDTH
_GATES_WIDTH = IN_WIDTH - _QKV_END

TM = 512
KV_HALO = BLOCK
CONV_HALO = 8
POOL_HALO = 32
FF_PIECE = 512
BF16_SUBLANES = 16
V7X_VMEM_LIMIT_BYTES = 60 * 1024 * 1024
_PHASE_ORDER = "m x m x m x m x m x m x m x m x x x x x m x x m m".split()

_F32 = jnp.float32
_BF16 = jnp.bfloat16


def _bucket_table():
    kj = np.arange(2 * BLOCK, dtype=np.int32)[:, None]
    qi = np.arange(BLOCK, dtype=np.int32)[None, :] + BLOCK
    dist = qi - kj
    n = np.maximum(dist, 0)
    max_exact = N_BUCKETS // 2
    nf = np.maximum(n, 1).astype(np.float32)
    large = max_exact + (np.log(nf / np.float32(max_exact)) / np.float32(math.log(MAX_DISTANCE / max_exact))
                         * np.float32(N_BUCKETS - max_exact)).astype(np.int32)
    large = np.minimum(large, N_BUCKETS - 1)
    bucket = np.where(n < max_exact, n, large)
    valid = (dist >= 0) & (dist < WINDOW)
    return np.where(valid, bucket, -1).astype(np.int32)


def _rmsnorm(x, g):
    ms = jnp.mean(x * x, axis=-1, keepdims=True)
    return x * lax.rsqrt(ms + EPS) * g


def _tables_kernel(relb_ref, poolw_ref, bkt_ref, bias_ref, wbd_ref):
    bkt = bkt_ref[...]
    for h in range(N_Q_HEADS):
        acc = jnp.where(bkt < 0, NEG, 0.0).astype(_F32)
        for b in range(N_BUCKETS):
            acc = jnp.where(bkt == b, relb_ref[b, h] * LOG2E, acc)
        g, gi = divmod(h, GROUP)
        bias_ref[g, :, gi * BLOCK:(gi + 1) * BLOCK] = acc
    rows = lax.broadcasted_iota(jnp.int32, (POOL_GROUP, POOL_WIDTH), 0)
    cols = lax.broadcasted_iota(jnp.int32, (POOL_GROUP, POOL_WIDTH), 1)
    rep = jnp.where((cols % POOL_GROUP) == rows, 1.0, 0.0).astype(_BF16)
    for l in range(poolw_ref.shape[0]):
        for g in range(len(POOL_WINDOWS)):
            tiled = jnp.dot(poolw_ref[l, g].astype(_BF16), rep, preferred_element_type=_F32)
            wbd_ref[l, g * POOL_GROUP:(g + 1) * POOL_GROUP, :] = jnp.where(cols // POOL_GROUP == g, tiled,
                                                                          0.0).astype(_BF16)


def _tables_call(relb, poolw, bkt):
    depth = poolw.shape[0]
    vmem = pl.BlockSpec(memory_space=pltpu.VMEM)
    return pl.pallas_call(
        _tables_kernel,
        in_specs=[pl.BlockSpec(memory_space=pltpu.SMEM), vmem, vmem],
        out_specs=[vmem, vmem],
        out_shape=[jax.ShapeDtypeStruct((N_KV_HEADS, 2 * BLOCK, GROUP * BLOCK), _F32),
                   jax.ShapeDtypeStruct((depth, POOL_WIDTH, POOL_WIDTH), _BF16)],
        name="hybrid_tables",
    )(relb, poolw, bkt)


def _attention_scores(n, g, first_tile, q_scr, k_scr, bias_scr, sink_row):
    r0 = n * BLOCK
    kband = k_scr[r0:r0 + 2 * BLOCK, :]
    zeros = jnp.zeros((HEAD_DIM, GROUP * BLOCK), _F32)
    scale = LOG2E / math.sqrt(HEAD_DIM)
    qg_t = (q_scr[r0:r0 + BLOCK, g * GROUP * HEAD_DIM:(g + 1) * GROUP * HEAD_DIM] * scale).T
    qcat = jnp.concatenate([qg_t[gi * HEAD_DIM:(gi + 1) * HEAD_DIM, :] for gi in range(GROUP)], axis=1)
    rhs = jnp.concatenate([qcat, zeros] if g == 0 else [zeros, qcat], axis=0).astype(_BF16)
    s_t = jnp.dot(kband, rhs, preferred_element_type=_F32) + bias_scr[g]
    if n == 0:
        s_t = jnp.concatenate([s_t[:BLOCK] + jnp.where(first_tile, NEG, 0.0), s_t[BLOCK:]], axis=0)
    m = jnp.maximum(jnp.max(s_t, axis=0, keepdims=True), sink_row)
    return jnp.exp2(s_t - m).astype(_BF16), jnp.exp2(sink_row - m)


def _attention_output(n, g, e, e_sink, v_scr, mixed_scr):
    r0 = n * BLOCK
    v_t = v_scr[r0:r0 + 2 * BLOCK, :].T[g * HEAD_DIM:(g + 1) * HEAD_DIM, :]
    v_aug = jnp.concatenate([v_t, jnp.ones((BF16_SUBLANES, 2 * BLOCK), _F32)], axis=0).astype(_BF16)
    o_aug = jnp.dot(v_aug, e, preferred_element_type=_F32)
    inv_denom = 1.0 / (o_aug[HEAD_DIM:HEAD_DIM + 1, :] + e_sink)
    o_t = o_aug[0:HEAD_DIM, :] * inv_denom
    o4 = jnp.concatenate([o_t[:, gi * BLOCK:(gi + 1) * BLOCK] for gi in range(GROUP)], axis=0)
    mixed_scr[r0:r0 + BLOCK, g * GROUP * HEAD_DIM:(g + 1) * GROUP * HEAD_DIM] = o4.T.astype(_BF16)


def _mlp_half(x1_ref, h2_ref, w1_ref, w2_ref, fnorm_ref, o_ref, act_scr):
    for lo in range(0, D_FF, FF_PIECE):
        hid = jnp.dot(h2_ref[...], w1_ref[:, lo:lo + FF_PIECE], preferred_element_type=_F32)
        act_scr[:, lo:lo + FF_PIECE] = jnp.square(jnp.maximum(hid, 0.0)).astype(_BF16)
        yield
    halfd = D_MODEL // 2
    parts = []
    for i in range(2):
        parts.append(jnp.dot(act_scr[...], w2_ref[:, i * halfd:(i + 1) * halfd], preferred_element_type=_F32))
        yield
    out = x1_ref[...] + jnp.concatenate(parts, axis=1)
    if fnorm_ref is not None:
        out = _rmsnorm(out, fnorm_ref[...])
    o_ref[0] = out
    yield


def _mixer_half(x_ref, x1_ref, h2_ref, seq_tile, n1_ref, win_ref, convw_ref, sinks_ref, pscale_ref, wout_ref,
                n2_ref, layer, q_scr, k_scr, v_scr, g_scr, u_scr, p_scr, a1_scr, a2_scr, a3_scr, mixed_scr,
                bias_scr, wbd_scr):
    first_tile = seq_tile == 0

    x = x_ref[0]
    h = _rmsnorm(x, n1_ref[...]).astype(_BF16)
    q_scr[...] = jnp.dot(h, win_ref[:, 0:ATTN_WIDTH], preferred_element_type=_F32)
    for r in range(0, TM, TM // 2):
        kv = jnp.dot(h[r:r + TM // 2], win_ref[:, ATTN_WIDTH:_QKV_END], preferred_element_type=_F32)
        k_scr[KV_HALO + r:KV_HALO + r + TM // 2, :] = kv[:, 0:KV_WIDTH].astype(_BF16)
        v_scr[KV_HALO + r:KV_HALO + r + TM // 2, :] = kv[:, KV_WIDTH:2 * KV_WIDTH]
    yield

    lane = lax.broadcasted_iota(jnp.int32, (1, GROUP * BLOCK), 1)
    sink_rows = []
    for g in range(N_KV_HEADS):
        row = jnp.full((1, GROUP * BLOCK), sinks_ref[layer, g * GROUP + GROUP - 1], _F32)
        for gi in range(GROUP - 2, -1, -1):
            row = jnp.where(lane < (gi + 1) * BLOCK, sinks_ref[layer, g * GROUP + gi], row)
        sink_rows.append(row * LOG2E)
    gates_half = _GATES_WIDTH // 2
    n_pairs = (TM // BLOCK) * N_KV_HEADS
    pending = None
    for k in range(n_pairs):
        n, g = divmod(k, N_KV_HEADS)
        if pending is not None:
            _attention_output(*pending, v_scr, mixed_scr)
        pending = (n, g) + _attention_scores(n, g, first_tile, q_scr, k_scr, bias_scr, sink_rows[g])
        yield
        i = k - (n_pairs - 2)
        if i >= 0:
            lo = _QKV_END + i * gates_half
            g_scr[:, i * gates_half:(i + 1) * gates_half] = jnp.dot(h, win_ref[:, lo:lo + gates_half],
                                                                    preferred_element_type=_F32)
            yield
    _attention_output(*pending, v_scr, mixed_scr)
    k_scr[0:KV_HALO, :] = k_scr[TM:TM + KV_HALO, :]
    v_scr[0:KV_HALO, :] = v_scr[TM:TM + KV_HALO, :]

    c0 = CONV_HALO
    u_scr[c0:c0 + TM, :] = g_scr[:, CONV_WIDTH:2 * CONV_WIDTH] * g_scr[:, 2 * CONV_WIDTH:3 * CONV_WIDTH]
    cw = convw_ref[...]
    y = (cw[0:1, :] * u_scr[c0 - 2:c0 - 2 + TM, :] + cw[1:2, :] * u_scr[c0 - 1:c0 - 1 + TM, :]
         + cw[2:3, :] * u_scr[c0:c0 + TM, :])
    mixed_scr[:, ATTN_WIDTH:ATTN_WIDTH + CONV_WIDTH] = (g_scr[:, 0:CONV_WIDTH] * y).astype(_BF16)
    u_scr[0:CONV_HALO, :] = u_scr[TM:TM + CONV_HALO, :]

    p0 = POOL_HALO
    pe = p0 + TM
    p = g_scr[:, 3 * CONV_WIDTH:_GATES_WIDTH]
    p_scr[p0:pe, :] = p
    a1_scr[8:pe, :] = p_scr[8:pe, :] + p_scr[7:pe - 1, :]
    a2_scr[16:pe, :] = a1_scr[16:pe, :] + a1_scr[14:pe - 2, :]
    a3_scr[24:pe, :] = a2_scr[24:pe, :] + a2_scr[20:pe - 4, :]
    s16 = a3_scr[p0:pe, :] + a3_scr[p0 - 8:pe - 8, :]
    plane = lax.broadcasted_iota(jnp.int32, (TM, POOL_WIDTH), 1)
    ssum = jnp.where(plane < POOL_GROUP, a1_scr[p0:pe, :],
                     jnp.where(plane < 2 * POOL_GROUP, a2_scr[p0:pe, :],
                               jnp.where(plane < 3 * POOL_GROUP, a3_scr[p0:pe, :], s16)))
    win = jnp.where(plane < POOL_GROUP, POOL_WINDOWS[0],
                    jnp.where(plane < 2 * POOL_GROUP, POOL_WINDOWS[1],
                              jnp.where(plane < 3 * POOL_GROUP, POOL_WINDOWS[2], POOL_WINDOWS[3])))
    t = seq_tile * TM + lax.broadcasted_iota(jnp.int32, (TM, POOL_WIDTH), 0)
    count = jnp.minimum(t + 1, win).astype(_F32)
    pooled = (ssum / count - p).astype(_BF16)
    p_scr[0:POOL_HALO, :] = p_scr[TM:TM + POOL_HALO, :]
    yield
    for r in range(0, TM, TM // 2):
        mixed_pool = jnp.dot(pooled[r:r + TM // 2], wbd_scr[...], preferred_element_type=_F32) * pscale_ref[...]
        mixed_scr[r:r + TM // 2, ATTN_WIDTH + CONV_WIDTH:D_MODEL] = mixed_pool.astype(_BF16)
    yield

    x1 = x + jnp.dot(mixed_scr[...], wout_ref[...], preferred_element_type=_F32)
    x1_ref[...] = x1
    h2_ref[...] = _rmsnorm(x1, n2_ref[...]).astype(_BF16)
    yield


def _step_rows(s, buf):
    rows = buf.shape[0]
    return pl.ds(pl.multiple_of(s * rows, rows), rows)


def _fetch_copies(s, layer, srcs, in_bufs, sems):
    return [pltpu.make_async_copy(src.at[layer + 1, _step_rows(s, buf), :], buf, sems.at[i])
            for i, (src, buf) in enumerate(zip(srcs, in_bufs))]


def _put_copies(s, dsts, out_bufs, sems):
    return [pltpu.make_async_copy(buf, dst.at[_step_rows(s, buf), :], sems.at[i])
            for i, (dst, buf) in enumerate(zip(dsts, out_bufs))]


_N_BASE_SCRATCH = 16


def _layer_kernel(x_ref, n1_ref, win_ref, convw_ref, sinks_ref, bias_scr, wbd_scr, pscale_ref, wout_ref,
                  n2_ref, w1_hbm, w2_hbm, *rest, layer, final, cast_next, tiles_per_seq):
    rest = list(rest)
    fnorm_ref = rest.pop(0) if final else None
    next_f32 = [rest.pop(0) for _ in range(4)] if cast_next else []
    o_ref = rest.pop(0)
    next_bf16 = [rest.pop(0) for _ in range(4)] if cast_next else []
    (x1_scr, h2_scr, act_scr, q_scr, k_scr, v_scr, g_scr, u_scr, p_scr, a1_scr, a2_scr, a3_scr, mixed_scr,
     w1_ref, w2_ref, mlp_w_sems) = rest[:_N_BASE_SCRATCH]
    s = pl.program_id(0)
    n_tiles = pl.num_programs(0) - 1
    slot = s % 2
    seq_tile = jnp.minimum(s, n_tiles - 1) % tiles_per_seq

    mlp_w_copies = [pltpu.make_async_copy(src, dst, mlp_w_sems.at[i])
                    for i, (src, dst) in enumerate(((w1_hbm, w1_ref), (w2_hbm, w2_ref)))]

    @pl.when(s == 0)
    def _():
        for copy in mlp_w_copies:
            copy.start()

    @pl.when(s == 1)
    def _():
        for copy in mlp_w_copies:
            copy.wait()

    if cast_next:
        cast_scratch = rest[_N_BASE_SCRATCH:]
        in_bufs, out_bufs, (in_sems, out_sems) = cast_scratch[0:4], cast_scratch[4:8], cast_scratch[8:10]

        @pl.when(s < n_tiles)
        def _():
            for fetch in _fetch_copies(s, layer, next_f32, in_bufs, in_sems):
                fetch.start()

    @pl.when(jnp.logical_and(seq_tile == 0, s < n_tiles))
    def _():
        k_scr[0:KV_HALO, :] = jnp.zeros((KV_HALO, KV_WIDTH), _BF16)
        v_scr[0:KV_HALO, :] = jnp.zeros((KV_HALO, KV_WIDTH), _F32)
        u_scr[0:CONV_HALO, :] = jnp.zeros((CONV_HALO, CONV_WIDTH), _F32)
        p_scr[0:POOL_HALO, :] = jnp.zeros((POOL_HALO, POOL_WIDTH), _F32)

    def mlp_half():
        return _mlp_half(x1_scr.at[1 - slot], h2_scr.at[1 - slot], w1_ref, w2_ref, fnorm_ref if final else None,
                         o_ref, act_scr)

    def mixer_half():
        this_layer = pl.ds(layer, 1)
        return _mixer_half(x_ref, x1_scr.at[slot], h2_scr.at[slot], seq_tile, n1_ref.at[this_layer], win_ref,
                           convw_ref, sinks_ref, pscale_ref.at[this_layer], wout_ref, n2_ref.at[this_layer], layer,
                           q_scr, k_scr, v_scr, g_scr, u_scr, p_scr,
                           a1_scr, a2_scr, a3_scr, mixed_scr, bias_scr, wbd_scr)

    @pl.when(s == 0)
    def _():
        for _ in mixer_half():
            pass

    @pl.when(jnp.logical_and(s > 0, s < n_tiles))
    def _():
        mlp, mixer = mlp_half(), mixer_half()
        for who in _PHASE_ORDER:
            next(mlp if who == "m" else mixer)
        for half in (mlp, mixer):
            assert next(half, "done") == "done", "phase order does not cover every phase"

    @pl.when(s == n_tiles)
    def _():
        for _ in mlp_half():
            pass

    if cast_next:
        @pl.when(s < n_tiles)
        def _():
            @pl.when(s > 0)
            def _():
                for put in _put_copies(s - 1, next_bf16, out_bufs, out_sems):
                    put.wait()

            for fetch, put, ibuf, obuf in zip(_fetch_copies(s, layer, next_f32, in_bufs, in_sems),
                                              _put_copies(s, next_bf16, out_bufs, out_sems), in_bufs, out_bufs):
                fetch.wait()
                obuf[...] = ibuf[...].astype(_BF16)
                put.start()

        @pl.when(s == n_tiles)
        def _():
            for put in _put_copies(s - 1, next_bf16, out_bufs, out_sems):
                put.wait()


def _resident(shape, layer=None):
    if layer is None:
        return pl.BlockSpec(shape, lambda s: (0,) * len(shape), pipeline_mode=pl.Buffered(1))
    return pl.BlockSpec((None,) + shape, lambda s: (layer,) + (0,) * len(shape), pipeline_mode=pl.Buffered(1))


def _layer_call(x, layer, n1, win, convw, sinks, bias, wbd, pscale, wout, n2, w1, w2, fnorm, next_f32):
    batch, seq, _ = x.shape
    assert seq % TM == 0 and TM % BLOCK == 0
    tiles_per_seq = seq // TM
    n_tiles = batch * tiles_per_seq
    final = fnorm is not None
    cast_next = next_f32 is not None
    smem = pl.BlockSpec(memory_space=pltpu.SMEM)
    hbm = pl.BlockSpec(memory_space=pl.ANY)

    def in_tile(s):
        t = jnp.minimum(s, n_tiles - 1)
        return (t // tiles_per_seq, t % tiles_per_seq, 0)

    def out_tile(s):
        t = jnp.maximum(s - 1, 0)
        return (t // tiles_per_seq, t % tiles_per_seq, 0)

    in_specs = [
        pl.BlockSpec((1, TM, D_MODEL), in_tile),
        _resident(n1.shape),
        _resident((D_MODEL, IN_WIDTH)),
        _resident((CONV_K, CONV_WIDTH), layer),
        smem,
        _resident((N_KV_HEADS, 2 * BLOCK, GROUP * BLOCK)),
        _resident((POOL_WIDTH, POOL_WIDTH), layer),
        _resident(pscale.shape),
        _resident((D_MODEL, D_MODEL)),
        _resident(n2.shape),
        hbm,
        hbm,
    ]
    args = [x, n1, win, convw, sinks, bias, wbd, pscale, wout, n2, w1, w2]
    if final:
        in_specs.append(_resident((1, D_MODEL)))
        args.append(fnorm)
    out_specs = [pl.BlockSpec((1, TM, D_MODEL), out_tile)]
    out_shape = [jax.ShapeDtypeStruct(x.shape, x.dtype)]
    cast_scratch = []
    if cast_next:
        in_specs += [hbm] * len(next_f32)
        args += list(next_f32)
        out_specs += [hbm] * len(next_f32)
        out_shape += [jax.ShapeDtypeStruct(w.shape[1:], _BF16) for w in next_f32]
        rows = [w.shape[1] // n_tiles for w in next_f32]
        assert all(r * n_tiles == w.shape[1] and r % BF16_SUBLANES == 0 for r, w in zip(rows, next_f32))
        cast_scratch = ([pltpu.VMEM((r, w.shape[2]), _F32) for r, w in zip(rows, next_f32)]
                        + [pltpu.VMEM((r, w.shape[2]), _BF16) for r, w in zip(rows, next_f32)]
                        + [pltpu.SemaphoreType.DMA((len(next_f32),))] * 2)
    scratch = [
        pltpu.VMEM((2, TM, D_MODEL), _F32),
        pltpu.VMEM((2, TM, D_MODEL), _BF16),
        pltpu.VMEM((TM, D_FF), _BF16),
        pltpu.VMEM((TM, ATTN_WIDTH), _F32),
        pltpu.VMEM((KV_HALO + TM, KV_WIDTH), _BF16),
        pltpu.VMEM((KV_HALO + TM, KV_WIDTH), _F32),
        pltpu.VMEM((TM, _GATES_WIDTH), _F32),
        pltpu.VMEM((CONV_HALO + TM, CONV_WIDTH), _F32),
        pltpu.VMEM((POOL_HALO + TM, POOL_WIDTH), _F32),
        pltpu.VMEM((POOL_HALO + TM, POOL_WIDTH), _F32),
        pltpu.VMEM((POOL_HALO + TM, POOL_WIDTH), _F32),
        pltpu.VMEM((POOL_HALO + TM, POOL_WIDTH), _F32),
        pltpu.VMEM((TM, D_MODEL), _BF16),
        pltpu.VMEM((D_MODEL, D_FF), _BF16),
        pltpu.VMEM((D_FF, D_MODEL), _BF16),
        pltpu.SemaphoreType.DMA((2,)),
    ]
    assert len(scratch) == _N_BASE_SCRATCH
    scratch += cast_scratch
    outs = pl.pallas_call(
        functools.partial(_layer_kernel, layer=layer, final=final, cast_next=cast_next,
                          tiles_per_seq=tiles_per_seq),
        grid=(n_tiles + 1,),
        in_specs=in_specs,
        out_specs=out_specs,
        out_shape=out_shape,
        scratch_shapes=scratch,
        compiler_params=pltpu.CompilerParams(
            dimension_semantics=("arbitrary",),
            vmem_limit_bytes=V7X_VMEM_LIMIT_BYTES,
        ),
        name="hybrid_layer_final" if final else "hybrid_layer",
    )(*args)
    return outs[0], outs[1:]


def kernel(x, norm1, w_in, conv_w, sinks, pool_w, pool_scale, w_out, norm2, w1, w2, rel_bias, final_norm):
    depth = w_in.shape[0]
    bkt = jnp.asarray(_bucket_table())
    bias, wbd = _tables_call(rel_bias, pool_w, bkt)
    weights_f32 = (w_in, w_out, w1, w2)
    win, wout, w1b, w2b = [w[0].astype(_BF16) for w in weights_f32]
    for l in range(depth):
        last = l == depth - 1
        fnorm = final_norm.reshape(1, D_MODEL) if last else None
        x, nxt = _layer_call(x, l, norm1, win, conv_w, sinks, bias, wbd, pool_scale, wout, norm2, w1b, w2b, fnorm,
                             None if last else weights_f32)
        if not last:
            win, wout, w1b, w2b = nxt
    return x
```

```python
import functools
import math

import numpy as np
import jax
import jax.numpy as jnp
from jax import lax
from jax.experimental import pallas as pl
from jax.experimental.pallas import tpu as pltpu

D_MODEL = 1024
HEAD_DIM = 64
ATTN_WIDTH = 512
CONV_WIDTH = 256
CONV_K = 3
POOL_WIDTH = 256
N_Q_HEADS = 8
N_KV_HEADS = 2
GROUP = N_Q_HEADS // N_KV_HEADS
KV_WIDTH = N_KV_HEADS * HEAD_DIM
POOL_WINDOWS = (2, 4, 8, 16)
POOL_GROUP = 64
IN_WIDTH = 1792
D_FF = 4096
WINDOW = 128
BLOCK = 128
N_BUCKETS = 32
MAX_DISTANCE = 128
EPS = 1e-6
NEG = -1e30
LOG2E = math.log2(math.e)
Q_SCALE = LOG2E / math.sqrt(HEAD_DIM)

_QKV_END = ATTN_WIDTH + 2 * KV_WIDTH
_GATES_WIDTH = IN_WIDTH - _QKV_END

TM = 512
KV_HALO = BLOCK
CONV_HALO = 8
POOL_HALO = 32
FF_PIECE = 512
BF16_SUBLANES = 16
V7X_VMEM_LIMIT_BYTES = 60 * 1024 * 1024
_PHASE_ORDER = "m x m x m x m x m x m x m x m x x x x x m x x m m".split()

_F32 = jnp.float32
_BF16 = jnp.bfloat16


def _bucket_table():
    kj = np.arange(2 * BLOCK, dtype=np.int32)[:, None]
    qi = np.arange(BLOCK, dtype=np.int32)[None, :] + BLOCK
    dist = qi - kj
    n = np.maximum(dist, 0)
    max_exact = N_BUCKETS // 2
    nf = np.maximum(n, 1).astype(np.float32)
    large = max_exact + (np.log(nf / np.float32(max_exact)) / np.float32(math.log(MAX_DISTANCE / max_exact))
                         * np.float32(N_BUCKETS - max_exact)).astype(np.int32)
    large = np.minimum(large, N_BUCKETS - 1)
    bucket = np.where(n < max_exact, n, large)
    valid = (dist >= 0) & (dist < WINDOW)
    return np.where(valid, bucket, -1).astype(np.int32)


def _rmsnorm(x, g):
    ms = jnp.mean(x * x, axis=-1, keepdims=True)
    return x * lax.rsqrt(ms + EPS) * g


def _tables_kernel(relb_ref, poolw_ref, bkt_ref, bias_ref, wbd_ref):
    bkt = bkt_ref[...]
    for h in range(N_Q_HEADS):
        acc = jnp.where(bkt < 0, NEG, 0.0).astype(_F32)
        for b in range(N_BUCKETS):
            acc = jnp.where(bkt == b, relb_ref[b, h] * LOG2E, acc)
        g, gi = divmod(h, GROUP)
        bias_ref[g, :, gi * BLOCK:(gi + 1) * BLOCK] = acc
    rows = lax.broadcasted_iota(jnp.int32, (POOL_GROUP, POOL_WIDTH), 0)
    cols = lax.broadcasted_iota(jnp.int32, (POOL_GROUP, POOL_WIDTH), 1)
    rep = jnp.where((cols % POOL_GROUP) == rows, 1.0, 0.0).astype(_BF16)
    for l in range(poolw_ref.shape[0]):
        for g in range(len(POOL_WINDOWS)):
            tiled = jnp.dot(poolw_ref[l, g].astype(_BF16), rep, preferred_element_type=_F32)
            wbd_ref[l, g * POOL_GROUP:(g + 1) * POOL_GROUP, :] = jnp.where(cols // POOL_GROUP == g, tiled,
                                                                          0.0).astype(_BF16)


def _tables_call(relb, poolw, bkt):
    depth = poolw.shape[0]
    vmem = pl.BlockSpec(memory_space=pltpu.VMEM)
    return pl.pallas_call(
        _tables_kernel,
        in_specs=[pl.BlockSpec(memory_space=pltpu.SMEM), vmem, vmem],
        out_specs=[vmem, vmem],
        out_shape=[jax.ShapeDtypeStruct((N_KV_HEADS, 2 * BLOCK, GROUP * BLOCK), _F32),
                   jax.ShapeDtypeStruct((depth, POOL_WIDTH, POOL_WIDTH), _BF16)],
        name="hybrid_tables",
    )(relb, poolw, bkt)


def _attention_scores(n, g, first_tile, q_scr, k_scr, bias_scr, sink_row):
    r0 = n * BLOCK
    kband = k_scr[r0:r0 + 2 * BLOCK, :]
    s_t = jnp.dot(kband, q_scr[n * N_KV_HEADS + g], preferred_element_type=_F32) + bias_scr[g]
    if n == 0:
        s_t = jnp.concatenate([s_t[:BLOCK] + jnp.where(first_tile, NEG, 0.0), s_t[BLOCK:]], axis=0)
    m = jnp.maximum(jnp.max(s_t, axis=0, keepdims=True), sink_row)
    return jnp.exp2(s_t - m).astype(_BF16), jnp.exp2(sink_row - m)


def _attention_output(n, g, e, e_sink, v_scr, mixed_scr):
    r0 = n * BLOCK
    v_t = v_scr[r0:r0 + 2 * BLOCK, :].T[g * HEAD_DIM:(g + 1) * HEAD_DIM, :]
    v_aug = jnp.concatenate([v_t, jnp.ones((BF16_SUBLANES, 2 * BLOCK), _F32)], axis=0).astype(_BF16)
    o_aug = jnp.dot(v_aug, e, preferred_element_type=_F32)
    inv_denom = 1.0 / (o_aug[HEAD_DIM:HEAD_DIM + 1, :] + e_sink)
    o_t = o_aug[0:HEAD_DIM, :] * inv_denom
    o4 = jnp.concatenate([o_t[:, gi * BLOCK:(gi + 1) * BLOCK] for gi in range(GROUP)], axis=0)
    mixed_scr[r0:r0 + BLOCK, g * GROUP * HEAD_DIM:(g + 1) * GROUP * HEAD_DIM] = o4.T.astype(_BF16)


def _mlp_half(x1_ref, h2_ref, w1_ref, w2_ref, fnorm_ref, o_ref, act_scr):
    for lo in range(0, D_FF, FF_PIECE):
        hid = jnp.dot(h2_ref[...], w1_ref[:, lo:lo + FF_PIECE], preferred_element_type=_F32)
        act_scr[:, lo:lo + FF_PIECE] = jnp.square(jnp.maximum(hid, 0.0)).astype(_BF16)
        yield
    halfd = D_MODEL // 2
    parts = []
    for i in range(2):
        parts.append(jnp.dot(act_scr[...], w2_ref[:, i * halfd:(i + 1) * halfd], preferred_element_type=_F32))
        yield
    out = x1_ref[...] + jnp.concatenate(parts, axis=1)
    if fnorm_ref is not None:
        out = _rmsnorm(out, fnorm_ref[...])
    o_ref[0] = out
    yield


def _mixer_half(x_ref, x1_ref, h2_ref, seq_tile, n1_ref, win_ref, convw_ref, sinks_ref, pscale_ref, wout_ref,
                n2_ref, layer, q_scr, k_scr, v_scr, g_scr, u_scr, p_scr, a1_scr, a2_scr, a3_scr, mixed_scr,
                bias_scr, wbd_scr):
    first_tile = seq_tile == 0

    x = x_ref[0]
    h = _rmsnorm(x, n1_ref[...]).astype(_BF16)
    q = jnp.dot(h, win_ref[:, 0:ATTN_WIDTH], preferred_element_type=_F32) * Q_SCALE
    zeros = jnp.zeros((HEAD_DIM, GROUP * BLOCK), _F32)
    for n in range(TM // BLOCK):
        for g in range(N_KV_HEADS):
            qg_t = q[n * BLOCK:(n + 1) * BLOCK, g * GROUP * HEAD_DIM:(g + 1) * GROUP * HEAD_DIM].T
            qcat = jnp.concatenate([qg_t[gi * HEAD_DIM:(gi + 1) * HEAD_DIM, :] for gi in range(GROUP)], axis=1)
            rhs = jnp.concatenate([qcat, zeros] if g == 0 else [zeros, qcat], axis=0)
            q_scr[n * N_KV_HEADS + g] = rhs.astype(_BF16)
    for r in range(0, TM, TM // 2):
        kv = jnp.dot(h[r:r + TM // 2], win_ref[:, ATTN_WIDTH:_QKV_END], preferred_element_type=_F32)
        k_scr[KV_HALO + r:KV_HALO + r + TM // 2, :] = kv[:, 0:KV_WIDTH].astype(_BF16)
        v_scr[KV_HALO + r:KV_HALO + r + TM // 2, :] = kv[:, KV_WIDTH:2 * KV_WIDTH]
    yield

    lane = lax.broadcasted_iota(jnp.int32, (1, GROUP * BLOCK), 1)
    sink_rows = []
    for g in range(N_KV_HEADS):
        row = jnp.full((1, GROUP * BLOCK), sinks_ref[layer, g * GROUP + GROUP - 1], _F32)
        for gi in range(GROUP - 2, -1, -1):
            row = jnp.where(lane < (gi + 1) * BLOCK, sinks_ref[layer, g * GROUP + gi], row)
        sink_rows.append(row * LOG2E)
    gates_half = _GATES_WIDTH // 2
    n_pairs = (TM // BLOCK) * N_KV_HEADS
    pending = None
    for k in range(n_pairs):
        n, g = divmod(k, N_KV_HEADS)
        if pending is not None:
            _attention_output(*pending, v_scr, mixed_scr)
        pending = (n, g) + _attention_scores(n, g, first_tile, q_scr, k_scr, bias_scr, sink_rows[g])
        yield
        i = k - (n_pairs - 2)
        if i >= 0:
            lo = _QKV_END + i * gates_half
            g_scr[:, i * gates_half:(i + 1) * gates_half] = jnp.dot(h, win_ref[:, lo:lo + gates_half],
                                                                    preferred_element_type=_F32)
            yield
    _attention_output(*pending, v_scr, mixed_scr)
    k_scr[0:KV_HALO, :] = k_scr[TM:TM + KV_HALO, :]
    v_scr[0:KV_HALO, :] = v_scr[TM:TM + KV_HALO, :]

    c0 = CONV_HALO
    u_scr[c0:c0 + TM, :] = g_scr[:, CONV_WIDTH:2 * CONV_WIDTH] * g_scr[:, 2 * CONV_WIDTH:3 * CONV_WIDTH]
    cw = convw_ref[...]
    y = (cw[0:1, :] * u_scr[c0 - 2:c0 - 2 + TM, :] + cw[1:2, :] * u_scr[c0 - 1:c0 - 1 + TM, :]
         + cw[2:3, :] * u_scr[c0:c0 + TM, :])
    mixed_scr[:, ATTN_WIDTH:ATTN_WIDTH + CONV_WIDTH] = (g_scr[:, 0:CONV_WIDTH] * y).astype(_BF16)
    u_scr[0:CONV_HALO, :] = u_scr[TM:TM + CONV_HALO, :]

    p0 = POOL_HALO
    pe = p0 + TM
    p = g_scr[:, 3 * CONV_WIDTH:_GATES_WIDTH]
    p_scr[p0:pe, :] = p
    a1_scr[8:pe, :] = p_scr[8:pe, :] + p_scr[7:pe - 1, :]
    a2_scr[16:pe, :] = a1_scr[16:pe, :] + a1_scr[14:pe - 2, :]
    hi = slice(2 * POOL_GROUP, POOL_WIDTH)
    a3_scr[24:pe, hi] = a2_scr[24:pe, hi] + a2_scr[20:pe - 4, hi]
    s16 = a3_scr[p0:pe, hi] + a3_scr[p0 - 8:pe - 8, hi]
    first_group = lax.broadcasted_iota(jnp.int32, (TM, 2 * POOL_GROUP), 1) < POOL_GROUP
    ssum = jnp.concatenate([jnp.where(first_group, a1_scr[p0:pe, 0:2 * POOL_GROUP], a2_scr[p0:pe, 0:2 * POOL_GROUP]),
                            jnp.where(first_group, a3_scr[p0:pe, hi], s16)], axis=1)
    head = max(POOL_WINDOWS)
    hlane = lax.broadcasted_iota(jnp.int32, (head, POOL_WIDTH), 1)
    win = jnp.where(hlane < POOL_GROUP, POOL_WINDOWS[0],
                    jnp.where(hlane < 2 * POOL_GROUP, POOL_WINDOWS[1],
                              jnp.where(hlane < 3 * POOL_GROUP, POOL_WINDOWS[2], POOL_WINDOWS[3])))
    t = seq_tile * TM + lax.broadcasted_iota(jnp.int32, (head, POOL_WIDTH), 0)
    inv_head = 1.0 / jnp.minimum(t + 1, win).astype(_F32)
    inv_count = jnp.concatenate([inv_head, jnp.broadcast_to(inv_head[head - 1:head, :], (TM - head, POOL_WIDTH))],
                                axis=0)
    pooled = (ssum * inv_count - p).astype(_BF16)
    p_scr[0:POOL_HALO, :] = p_scr[TM:TM + POOL_HALO, :]
    yield
    for r in range(0, TM, TM // 2):
        mixed_pool = jnp.dot(pooled[r:r + TM // 2], wbd_scr[...], preferred_element_type=_F32) * pscale_ref[...]
        mixed_scr[r:r + TM // 2, ATTN_WIDTH + CONV_WIDTH:D_MODEL] = mixed_pool.astype(_BF16)
    yield

    x1 = x + jnp.dot(mixed_scr[...], wout_ref[...], preferred_element_type=_F32)
    x1_ref[...] = x1
    h2_ref[...] = _rmsnorm(x1, n2_ref[...]).astype(_BF16)
    yield


def _step_rows(s, buf):
    rows = buf.shape[0]
    return pl.ds(pl.multiple_of(s * rows, rows), rows)


def _fetch_copies(s, layer, srcs, in_bufs, sems):
    return [pltpu.make_async_copy(src.at[layer + 1, _step_rows(s, buf), :], buf, sems.at[i])
            for i, (src, buf) in enumerate(zip(srcs, in_bufs))]


def _put_copies(s, dsts, out_bufs, sems):
    return [pltpu.make_async_copy(buf, dst.at[_step_rows(s, buf), :], sems.at[i])
            for i, (dst, buf) in enumerate(zip(dsts, out_bufs))]


_N_BASE_SCRATCH = 16


def _layer_kernel(x_ref, n1_ref, win_ref, convw_ref, sinks_ref, bias_scr, wbd_scr, pscale_ref, wout_ref,
                  n2_ref, w1_hbm, w2_hbm, *rest, layer, final, cast_next, tiles_per_seq):
    rest = list(rest)
    fnorm_ref = rest.pop(0) if final else None
    next_f32 = [rest.pop(0) for _ in range(4)] if cast_next else []
    o_ref = rest.pop(0)
    next_bf16 = [rest.pop(0) for _ in range(4)] if cast_next else []
    (x1_scr, h2_scr, act_scr, q_scr, k_scr, v_scr, g_scr, u_scr, p_scr, a1_scr, a2_scr, a3_scr, mixed_scr,
     w1_ref, w2_ref, mlp_w_sems) = rest[:_N_BASE_SCRATCH]
    s = pl.program_id(0)
    n_tiles = pl.num_programs(0) - 1
    slot = s % 2
    seq_tile = jnp.minimum(s, n_tiles - 1) % tiles_per_seq

    mlp_w_copies = [pltpu.make_async_copy(src, dst, mlp_w_sems.at[i])
                    for i, (src, dst) in enumerate(((w1_hbm, w1_ref), (w2_hbm, w2_ref)))]

    @pl.when(s == 0)
    def _():
        for copy in mlp_w_copies:
            copy.start()

    @pl.when(s == 1)
    def _():
        for copy in mlp_w_copies:
            copy.wait()

    if cast_next:
        cast_scratch = rest[_N_BASE_SCRATCH:]
        in_bufs, out_bufs, (in_sems, out_sems) = cast_scratch[0:4], cast_scratch[4:8], cast_scratch[8:10]

        @pl.when(s < n_tiles)
        def _():
            for fetch in _fetch_copies(s, layer, next_f32, in_bufs, in_sems):
                fetch.start()

    @pl.when(jnp.logical_and(seq_tile == 0, s < n_tiles))
    def _():
        k_scr[0:KV_HALO, :] = jnp.zeros((KV_HALO, KV_WIDTH), _BF16)
        v_scr[0:KV_HALO, :] = jnp.zeros((KV_HALO, KV_WIDTH), _F32)
        u_scr[0:CONV_HALO, :] = jnp.zeros((CONV_HALO, CONV_WIDTH), _F32)
        p_scr[0:POOL_HALO, :] = jnp.zeros((POOL_HALO, POOL_WIDTH), _F32)

    def mlp_half():
        return _mlp_half(x1_scr.at[1 - slot], h2_scr.at[1 - slot], w1_ref, w2_ref, fnorm_ref if final else None,
                         o_ref, act_scr)

    def mixer_half():
        this_layer = pl.ds(layer, 1)
        return _mixer_half(x_ref, x1_scr.at[slot], h2_scr.at[slot], seq_tile, n1_ref.at[this_layer], win_ref,
                           convw_ref, sinks_ref, pscale_ref.at[this_layer], wout_ref, n2_ref.at[this_layer], layer,
                           q_scr, k_scr, v_scr, g_scr, u_scr, p_scr, a1_scr, a2_scr, a3_scr, mixed_scr,
                           bias_scr, wbd_scr)

    @pl.when(s == 0)
    def _():
        for _ in mixer_half():
            pass

    @pl.when(jnp.logical_and(s > 0, s < n_tiles))
    def _():
        mlp, mixer = mlp_half(), mixer_half()
        for who in _PHASE_ORDER:
            next(mlp if who == "m" else mixer)
        for half in (mlp, mixer):
            assert next(half, "done") == "done", "phase order does not cover every phase"

    @pl.when(s == n_tiles)
    def _():
        for _ in mlp_half():
            pass

    if cast_next:
        @pl.when(s < n_tiles)
        def _():
            @pl.when(s > 0)
            def _():
                for put in _put_copies(s - 1, next_bf16, out_bufs, out_sems):
                    put.wait()

            for fetch, put, ibuf, obuf in zip(_fetch_copies(s, layer, next_f32, in_bufs, in_sems),
                                              _put_copies(s, next_bf16, out_bufs, out_sems), in_bufs, out_bufs):
                fetch.wait()
                obuf[...] = ibuf[...].astype(_BF16)
                put.start()

        @pl.when(s == n_tiles)
        def _():
            for put in _put_copies(s - 1, next_bf16, out_bufs, out_sems):
                put.wait()


def _resident(shape, layer=None):
    if layer is None:
        return pl.BlockSpec(shape, lambda s: (0,) * len(shape), pipeline_mode=pl.Buffered(1))
    return pl.BlockSpec((None,) + shape, lambda s: (layer,) + (0,) * len(shape), pipeline_mode=pl.Buffered(1))


def _layer_call(x, layer, n1, win, convw, sinks, bias, wbd, pscale, wout, n2, w1, w2, fnorm, next_f32):
    batch, seq, _ = x.shape
    assert seq % TM == 0 and TM % BLOCK == 0
    tiles_per_seq = seq // TM
    n_tiles = batch * tiles_per_seq
    final = fnorm is not None
    cast_next = next_f32 is not None
    smem = pl.BlockSpec(memory_space=pltpu.SMEM)
    hbm = pl.BlockSpec(memory_space=pl.ANY)

    def in_tile(s):
        t = jnp.minimum(s, n_tiles - 1)
        return (t // tiles_per_seq, t % tiles_per_seq, 0)

    def out_tile(s):
        t = jnp.maximum(s - 1, 0)
        return (t // tiles_per_seq, t % tiles_per_seq, 0)

    in_specs = [
        pl.BlockSpec((1, TM, D_MODEL), in_tile),
        _resident(n1.shape),
        _resident((D_MODEL, IN_WIDTH)),
        _resident((CONV_K, CONV_WIDTH), layer),
        smem,
        _resident((N_KV_HEADS, 2 * BLOCK, GROUP * BLOCK)),
        _resident((POOL_WIDTH, POOL_WIDTH), layer),
        _resident(pscale.shape),
        _resident((D_MODEL, D_MODEL)),
        _resident(n2.shape),
        hbm,
        hbm,
    ]
    args = [x, n1, win, convw, sinks, bias, wbd, pscale, wout, n2, w1, w2]
    if final:
        in_specs.append(_resident((1, D_MODEL)))
        args.append(fnorm)
    out_specs = [pl.BlockSpec((1, TM, D_MODEL), out_tile)]
    out_shape = [jax.ShapeDtypeStruct(x.shape, x.dtype)]
    cast_scratch = []
    if cast_next:
        in_specs += [hbm] * len(next_f32)
        args += list(next_f32)
        out_specs += [hbm] * len(next_f32)
        out_shape += [jax.ShapeDtypeStruct(w.shape[1:], _BF16) for w in next_f32]
        rows = [w.shape[1] // n_tiles for w in next_f32]
        assert all(r * n_tiles == w.shape[1] and r % BF16_SUBLANES == 0 for r, w in zip(rows, next_f32))
        cast_scratch = ([pltpu.VMEM((r, w.shape[2]), _F32) for r, w in zip(rows, next_f32)]
                        + [pltpu.VMEM((r, w.shape[2]), _BF16) for r, w in zip(rows, next_f32)]
                        + [pltpu.SemaphoreType.DMA((len(next_f32),))] * 2)
    scratch = [
        pltpu.VMEM((2, TM, D_MODEL), _F32),
        pltpu.VMEM((2, TM, D_MODEL), _BF16),
        pltpu.VMEM((TM, D_FF), _BF16),
        pltpu.VMEM(((TM // BLOCK) * N_KV_HEADS, KV_WIDTH, GROUP * BLOCK), _BF16),
        pltpu.VMEM((KV_HALO + TM, KV_WIDTH), _BF16),
        pltpu.VMEM((KV_HALO + TM, KV_WIDTH), _F32),
        pltpu.VMEM((TM, _GATES_WIDTH), _F32),
        pltpu.VMEM((CONV_HALO + TM, CONV_WIDTH), _F32),
        pltpu.VMEM((POOL_HALO + TM, POOL_WIDTH), _F32),
        pltpu.VMEM((POOL_HALO + TM, POOL_WIDTH), _F32),
        pltpu.VMEM((POOL_HALO + TM, POOL_WIDTH), _F32),
        pltpu.VMEM((POOL_HALO + TM, POOL_WIDTH), _F32),
        pltpu.VMEM((TM, D_MODEL), _BF16),
        pltpu.VMEM((D_MODEL, D_FF), _BF16),
        pltpu.VMEM((D_FF, D_MODEL), _BF16),
        pltpu.SemaphoreType.DMA((2,)),
    ]
    assert len(scratch) == _N_BASE_SCRATCH
    scratch += cast_scratch
    outs = pl.pallas_call(
        functools.partial(_layer_kernel, layer=layer, final=final, cast_next=cast_next,
                          tiles_per_seq=tiles_per_seq),
        grid=(n_tiles + 1,),
        in_specs=in_specs,
        out_specs=out_specs,
        out_shape=out_shape,
        scratch_shapes=scratch,
        compiler_params=pltpu.CompilerParams(
            dimension_semantics=("arbitrary",),
            vmem_limit_bytes=V7X_VMEM_LIMIT_BYTES,
        ),
        name="hybrid_layer_final" if final else "hybrid_layer",
    )(*args)
    return outs[0], outs[1:]


def kernel(x, norm1, w_in, conv_w, sinks, pool_w, pool_scale, w_out, norm2, w1, w2, rel_bias, final_norm):
    depth = w_in.shape[0]
    bkt = jnp.asarray(_bucket_table())
    bias, wbd = _tables_call(rel_bias, pool_w, bkt)
    weights_f32 = (w_in, w_out, w1, w2)
    win, wout, w1b, w2b = [w[0].astype(_BF16) for w in weights_f32]
    for l in range(depth):
        last = l == depth - 1
        fnorm = final_norm.reshape(1, D_MODEL) if last else None
        x, nxt = _layer_call(x, l, norm1, win, conv_w, sinks, bias, wbd, pool_scale, wout, norm2, w1b, w2b, fnorm,
                             None if last else weights_f32)
        if not last:
            win, wout, w1b, w2b = nxt
    return x
```

```python
import functools
import math

import numpy as np
import jax
import jax.numpy as jnp
from jax import lax
from jax.experimental import pallas as pl
from jax.experimental.pallas import tpu as pltpu

D_MODEL = 1024
HEAD_DIM = 64
ATTN_WIDTH = 512
CONV_WIDTH = 256
CONV_K = 3
POOL_WIDTH = 256
N_Q_HEADS = 8
N_KV_HEADS = 2
GROUP = N_Q_HEADS // N_KV_HEADS
KV_WIDTH = N_KV_HEADS * HEAD_DIM
POOL_WINDOWS = (2, 4, 8, 16)
POOL_GROUP = 64
IN_WIDTH = 1792
D_FF = 4096
WINDOW = 128
BLOCK = 128
N_BUCKETS = 32
MAX_DISTANCE = 128
EPS = 1e-6
NEG = -1e30
LOG2E = math.log2(math.e)
Q_SCALE = LOG2E / math.sqrt(HEAD_DIM)

_QKV_END = ATTN_WIDTH + 2 * KV_WIDTH
_GATES_WIDTH = IN_WIDTH - _QKV_END

TM = 512
KV_HALO = BLOCK
CONV_HALO = 8
POOL_HALO = 32
FF_PIECE = 512
BF16_SUBLANES = 16
LANES = 128
V7X_VMEM_LIMIT_BYTES = 60 * 1024 * 1024
_PHASE_ORDER = "m x m x m x m x m x m x m x m x x x x x m x x m m".split()

_F32 = jnp.float32
_BF16 = jnp.bfloat16


def _bucket_table():
    kj = np.arange(2 * BLOCK, dtype=np.int32)[:, None]
    qi = np.arange(BLOCK, dtype=np.int32)[None, :] + BLOCK
    dist = qi - kj
    n = np.maximum(dist, 0)
    max_exact = N_BUCKETS // 2
    nf = np.maximum(n, 1).astype(np.float32)
    large = max_exact + (np.log(nf / np.float32(max_exact)) / np.float32(math.log(MAX_DISTANCE / max_exact))
                         * np.float32(N_BUCKETS - max_exact)).astype(np.int32)
    large = np.minimum(large, N_BUCKETS - 1)
    bucket = np.where(n < max_exact, n, large)
    valid = (dist >= 0) & (dist < WINDOW)
    return np.where(valid, bucket, -1).astype(np.int32)


def _rmsnorm(x, g):
    ms = jnp.mean(x * x, axis=-1)
    r = lax.rsqrt(ms.reshape(-1, LANES) + EPS).reshape(-1, 1)
    return x * r * g


def _tables_kernel(relb_ref, poolw_ref, bkt_ref, bias_ref, wbd_ref):
    bkt = bkt_ref[...]
    for h in range(N_Q_HEADS):
        acc = jnp.where(bkt < 0, NEG, 0.0).astype(_F32)
        for b in range(N_BUCKETS):
            acc = jnp.where(bkt == b, relb_ref[b, h] * LOG2E, acc)
        g, gi = divmod(h, GROUP)
        bias_ref[g, :, gi * BLOCK:(gi + 1) * BLOCK] = acc
    rows = lax.broadcasted_iota(jnp.int32, (POOL_GROUP, POOL_WIDTH), 0)
    cols = lax.broadcasted_iota(jnp.int32, (POOL_GROUP, POOL_WIDTH), 1)
    rep = jnp.where((cols % POOL_GROUP) == rows, 1.0, 0.0).astype(_BF16)
    for l in range(poolw_ref.shape[0]):
        for g in range(len(POOL_WINDOWS)):
            tiled = jnp.dot(poolw_ref[l, g].astype(_BF16), rep, preferred_element_type=_F32)
            wbd_ref[l, g * POOL_GROUP:(g + 1) * POOL_GROUP, :] = jnp.where(cols // POOL_GROUP == g, tiled,
                                                                          0.0).astype(_BF16)


def _tables_call(relb, poolw, bkt):
    depth = poolw.shape[0]
    vmem = pl.BlockSpec(memory_space=pltpu.VMEM)
    return pl.pallas_call(
        _tables_kernel,
        in_specs=[pl.BlockSpec(memory_space=pltpu.SMEM), vmem, vmem],
        out_specs=[vmem, vmem],
        out_shape=[jax.ShapeDtypeStruct((N_KV_HEADS, 2 * BLOCK, GROUP * BLOCK), _F32),
                   jax.ShapeDtypeStruct((depth, POOL_WIDTH, POOL_WIDTH), _BF16)],
        name="hybrid_tables",
    )(relb, poolw, bkt)


def _attention_scores(n, g, first_tile, q_scr, k_scr, bias_scr, sink_row):
    r0 = n * BLOCK
    kband = k_scr[r0:r0 + 2 * BLOCK, :]
    s_t = jnp.dot(kband, q_scr[n * N_KV_HEADS + g], preferred_element_type=_F32) + bias_scr[g]
    if n == 0:
        s_t = jnp.concatenate([s_t[:BLOCK] + jnp.where(first_tile, NEG, 0.0), s_t[BLOCK:]], axis=0)
    m = jnp.maximum(jnp.max(s_t, axis=0, keepdims=True), sink_row)
    return jnp.exp2(s_t - m).astype(_BF16), jnp.exp2(sink_row - m)


def _attention_output(n, g, e, e_sink, v_scr, mixed_scr):
    r0 = n * BLOCK
    v_t = v_scr[r0:r0 + 2 * BLOCK, :].T[g * HEAD_DIM:(g + 1) * HEAD_DIM, :]
    v_aug = jnp.concatenate([v_t, jnp.ones((BF16_SUBLANES, 2 * BLOCK), _F32)], axis=0).astype(_BF16)
    o_aug = jnp.dot(v_aug, e, preferred_element_type=_F32)
    inv_denom = 1.0 / (o_aug[HEAD_DIM:HEAD_DIM + 1, :] + e_sink)
    o_t = o_aug[0:HEAD_DIM, :] * inv_denom
    o4 = jnp.concatenate([o_t[:, gi * BLOCK:(gi + 1) * BLOCK] for gi in range(GROUP)], axis=0)
    mixed_scr[r0:r0 + BLOCK, g * GROUP * HEAD_DIM:(g + 1) * GROUP * HEAD_DIM] = o4.T.astype(_BF16)


def _mlp_half(x1_ref, h2_ref, w1_ref, w2_ref, fnorm_ref, o_ref, act_scr):
    for lo in range(0, D_FF, FF_PIECE):
        hid = jnp.dot(h2_ref[...], w1_ref[:, lo:lo + FF_PIECE], preferred_element_type=_F32)
        act_scr[:, lo:lo + FF_PIECE] = jnp.square(jnp.maximum(hid, 0.0)).astype(_BF16)
        yield
    halfd = D_MODEL // 2
    parts = []
    for i in range(2):
        parts.append(jnp.dot(act_scr[...], w2_ref[:, i * halfd:(i + 1) * halfd], preferred_element_type=_F32))
        yield
    out = x1_ref[...] + jnp.concatenate(parts, axis=1)
    if fnorm_ref is not None:
        out = _rmsnorm(out, fnorm_ref[...])
    o_ref[0] = out
    yield


def _mixer_half(x_ref, x1_ref, h2_ref, seq_tile, n1_ref, win_ref, convw_ref, sinks_ref, pscale_ref, wout_ref,
                n2_ref, layer, q_scr, k_scr, v_scr, g_scr, u_scr, p_scr, a1_scr, a2_scr, a3_scr, mixed_scr,
                bias_scr, wbd_scr):
    first_tile = seq_tile == 0

    x = x_ref[0]
    h = _rmsnorm(x, n1_ref[...]).astype(_BF16)
    q = jnp.dot(h, win_ref[:, 0:ATTN_WIDTH], preferred_element_type=_F32) * Q_SCALE
    zeros = jnp.zeros((HEAD_DIM, GROUP * BLOCK), _F32)
    for n in range(TM // BLOCK):
        for g in range(N_KV_HEADS):
            qg_t = q[n * BLOCK:(n + 1) * BLOCK, g * GROUP * HEAD_DIM:(g + 1) * GROUP * HEAD_DIM].T
            qcat = jnp.concatenate([qg_t[gi * HEAD_DIM:(gi + 1) * HEAD_DIM, :] for gi in range(GROUP)], axis=1)
            rhs = jnp.concatenate([qcat, zeros] if g == 0 else [zeros, qcat], axis=0)
            q_scr[n * N_KV_HEADS + g] = rhs.astype(_BF16)
    for r in range(0, TM, TM // 2):
        kv = jnp.dot(h[r:r + TM // 2], win_ref[:, ATTN_WIDTH:_QKV_END], preferred_element_type=_F32)
        k_scr[KV_HALO + r:KV_HALO + r + TM // 2, :] = kv[:, 0:KV_WIDTH].astype(_BF16)
        v_scr[KV_HALO + r:KV_HALO + r + TM // 2, :] = kv[:, KV_WIDTH:2 * KV_WIDTH]
    yield

    lane = lax.broadcasted_iota(jnp.int32, (1, GROUP * BLOCK), 1)
    sink_rows = []
    for g in range(N_KV_HEADS):
        row = jnp.full((1, GROUP * BLOCK), sinks_ref[layer, g * GROUP + GROUP - 1], _F32)
        for gi in range(GROUP - 2, -1, -1):
            row = jnp.where(lane < (gi + 1) * BLOCK, sinks_ref[layer, g * GROUP + gi], row)
        sink_rows.append(row * LOG2E)
    gates_half = _GATES_WIDTH // 2
    n_pairs = (TM // BLOCK) * N_KV_HEADS
    pending = None
    for k in range(n_pairs):
        n, g = divmod(k, N_KV_HEADS)
        if pending is not None:
            _attention_output(*pending, v_scr, mixed_scr)
        pending = (n, g) + _attention_scores(n, g, first_tile, q_scr, k_scr, bias_scr, sink_rows[g])
        yield
        i = k - (n_pairs - 2)
        if i >= 0:
            lo = _QKV_END + i * gates_half
            g_scr[:, i * gates_half:(i + 1) * gates_half] = jnp.dot(h, win_ref[:, lo:lo + gates_half],
                                                                    preferred_element_type=_F32)
            yield
    _attention_output(*pending, v_scr, mixed_scr)
    k_scr[0:KV_HALO, :] = k_scr[TM:TM + KV_HALO, :]
    v_scr[0:KV_HALO, :] = v_scr[TM:TM + KV_HALO, :]

    c0 = CONV_HALO
    u_scr[c0:c0 + TM, :] = g_scr[:, CONV_WIDTH:2 * CONV_WIDTH] * g_scr[:, 2 * CONV_WIDTH:3 * CONV_WIDTH]
    cw = convw_ref[...]
    y = (cw[0:1, :] * u_scr[c0 - 2:c0 - 2 + TM, :] + cw[1:2, :] * u_scr[c0 - 1:c0 - 1 + TM, :]
         + cw[2:3, :] * u_scr[c0:c0 + TM, :])
    mixed_scr[:, ATTN_WIDTH:ATTN_WIDTH + CONV_WIDTH] = (g_scr[:, 0:CONV_WIDTH] * y).astype(_BF16)
    u_scr[0:CONV_HALO, :] = u_scr[TM:TM + CONV_HALO, :]

    p0 = POOL_HALO
    pe = p0 + TM
    p = g_scr[:, 3 * CONV_WIDTH:_GATES_WIDTH]
    p_scr[p0:pe, :] = p
    a1_scr[8:pe, :] = p_scr[8:pe, :] + p_scr[7:pe - 1, :]
    a2_scr[16:pe, :] = a1_scr[16:pe, :] + a1_scr[14:pe - 2, :]
    hi = slice(2 * POOL_GROUP, POOL_WIDTH)
    a3_scr[24:pe, hi] = a2_scr[24:pe, hi] + a2_scr[20:pe - 4, hi]
    s16 = a3_scr[p0:pe, hi] + a3_scr[p0 - 8:pe - 8, hi]
    first_group = lax.broadcasted_iota(jnp.int32, (TM, 2 * POOL_GROUP), 1) < POOL_GROUP
    ssum = jnp.concatenate([jnp.where(first_group, a1_scr[p0:pe, 0:2 * POOL_GROUP], a2_scr[p0:pe, 0:2 * POOL_GROUP]),
                            jnp.where(first_group, a3_scr[p0:pe, hi], s16)], axis=1)
    head = max(POOL_WINDOWS)
    hlane = lax.broadcasted_iota(jnp.int32, (head, POOL_WIDTH), 1)
    win = jnp.where(hlane < POOL_GROUP, POOL_WINDOWS[0],
                    jnp.where(hlane < 2 * POOL_GROUP, POOL_WINDOWS[1],
                              jnp.where(hlane < 3 * POOL_GROUP, POOL_WINDOWS[2], POOL_WINDOWS[3])))
    t = seq_tile * TM + lax.broadcasted_iota(jnp.int32, (head, POOL_WIDTH), 0)
    inv_head = 1.0 / jnp.minimum(t + 1, win).astype(_F32)
    inv_count = jnp.concatenate([inv_head, jnp.broadcast_to(inv_head[head - 1:head, :], (TM - head, POOL_WIDTH))],
                                axis=0)
    pooled = (ssum * inv_count - p).astype(_BF16)
    p_scr[0:POOL_HALO, :] = p_scr[TM:TM + POOL_HALO, :]
    yield
    for r in range(0, TM, TM // 2):
        mixed_pool = jnp.dot(pooled[r:r + TM // 2], wbd_scr[...], preferred_element_type=_F32) * pscale_ref[...]
        mixed_scr[r:r + TM // 2, ATTN_WIDTH + CONV_WIDTH:D_MODEL] = mixed_pool.astype(_BF16)
    yield

    x1 = x + jnp.dot(mixed_scr[...], wout_ref[...], preferred_element_type=_F32)
    x1_ref[...] = x1
    h2_ref[...] = _rmsnorm(x1, n2_ref[...]).astype(_BF16)
    yield


def _step_rows(s, buf):
    rows = buf.shape[0]
    return pl.ds(pl.multiple_of(s * rows, rows), rows)


def _fetch_copies(s, layer, srcs, in_bufs, sems):
    return [pltpu.make_async_copy(src.at[layer + 1, _step_rows(s, buf), :], buf, sems.at[i])
            for i, (src, buf) in enumerate(zip(srcs, in_bufs))]


def _put_copies(s, dsts, out_bufs, sems):
    return [pltpu.make_async_copy(buf, dst.at[_step_rows(s, buf), :], sems.at[i])
            for i, (dst, buf) in enumerate(zip(dsts, out_bufs))]


_N_BASE_SCRATCH = 16


def _layer_kernel(x_ref, n1_ref, win_ref, convw_ref, sinks_ref, bias_scr, wbd_scr, pscale_ref, wout_ref,
                  n2_ref, w1_hbm, w2_hbm, *rest, layer, final, cast_next, tiles_per_seq):
    rest = list(rest)
    fnorm_ref = rest.pop(0) if final else None
    next_f32 = [rest.pop(0) for _ in range(4)] if cast_next else []
    o_ref = rest.pop(0)
    next_bf16 = [rest.pop(0) for _ in range(4)] if cast_next else []
    (x1_scr, h2_scr, act_scr, q_scr, k_scr, v_scr, g_scr, u_scr, p_scr, a1_scr, a2_scr, a3_scr, mixed_scr,
     w1_ref, w2_ref, mlp_w_sems) = rest[:_N_BASE_SCRATCH]
    s = pl.program_id(0)
    n_tiles = pl.num_programs(0) - 1
    slot = s % 2
    seq_tile = jnp.minimum(s, n_tiles - 1) % tiles_per_seq

    mlp_w_copies = [pltpu.make_async_copy(src, dst, mlp_w_sems.at[i])
                    for i, (src, dst) in enumerate(((w1_hbm, w1_ref), (w2_hbm, w2_ref)))]

    @pl.when(s == 0)
    def _():
        for copy in mlp_w_copies:
            copy.start()

    @pl.when(s == 1)
    def _():
        for copy in mlp_w_copies:
            copy.wait()

    if cast_next:
        cast_scratch = rest[_N_BASE_SCRATCH:]
        in_bufs, out_bufs, (in_sems, out_sems) = cast_scratch[0:4], cast_scratch[4:8], cast_scratch[8:10]

        @pl.when(s < n_tiles)
        def _():
            for fetch in _fetch_copies(s, layer, next_f32, in_bufs, in_sems):
                fetch.start()

    @pl.when(jnp.logical_and(seq_tile == 0, s < n_tiles))
    def _():
        k_scr[0:KV_HALO, :] = jnp.zeros((KV_HALO, KV_WIDTH), _BF16)
        v_scr[0:KV_HALO, :] = jnp.zeros((KV_HALO, KV_WIDTH), _F32)
        u_scr[0:CONV_HALO, :] = jnp.zeros((CONV_HALO, CONV_WIDTH), _F32)
        p_scr[0:POOL_HALO, :] = jnp.zeros((POOL_HALO, POOL_WIDTH), _F32)

    def mlp_half():
        return _mlp_half(x1_scr.at[1 - slot], h2_scr.at[1 - slot], w1_ref, w2_ref, fnorm_ref if final else None,
                         o_ref, act_scr)

    def mixer_half():
        this_layer = pl.ds(layer, 1)
        return _mixer_half(x_ref, x1_scr.at[slot], h2_scr.at[slot], seq_tile, n1_ref.at[this_layer], win_ref,
                           convw_ref, sinks_ref, pscale_ref.at[this_layer], wout_ref, n2_ref.at[this_layer], layer,
                           q_scr, k_scr, v_scr, g_scr, u_scr, p_scr, a1_scr, a2_scr, a3_scr, mixed_scr,
                           bias_scr, wbd_scr)

    @pl.when(s == 0)
    def _():
        for _ in mixer_half():
            pass

    @pl.when(jnp.logical_and(s > 0, s < n_tiles))
    def _():
        mlp, mixer = mlp_half(), mixer_half()
        for who in _PHASE_ORDER:
            next(mlp if who == "m" else mixer)
        for half in (mlp, mixer):
            assert next(half, "done") == "done", "phase order does not cover every phase"

    @pl.when(s == n_tiles)
    def _():
        for _ in mlp_half():
            pass

    if cast_next:
        @pl.when(s < n_tiles)
        def _():
            @pl.when(s > 0)
            def _():
                for put in _put_copies(s - 1, next_bf16, out_bufs, out_sems):
                    put.wait()

            for fetch, put, ibuf, obuf in zip(_fetch_copies(s, layer, next_f32, in_bufs, in_sems),
                                              _put_copies(s, next_bf16, out_bufs, out_sems), in_bufs, out_bufs):
                fetch.wait()
                obuf[...] = ibuf[...].astype(_BF16)
                put.start()

        @pl.when(s == n_tiles)
        def _():
            for put in _put_copies(s - 1, next_bf16, out_bufs, out_sems):
                put.wait()


def _resident(shape, layer=None):
    if layer is None:
        return pl.BlockSpec(shape, lambda s: (0,) * len(shape), pipeline_mode=pl.Buffered(1))
    return pl.BlockSpec((None,) + shape, lambda s: (layer,) + (0,) * len(shape), pipeline_mode=pl.Buffered(1))


def _layer_call(x, layer, n1, win, convw, sinks, bias, wbd, pscale, wout, n2, w1, w2, fnorm, next_f32):
    batch, seq, _ = x.shape
    assert seq % TM == 0 and TM % BLOCK == 0
    tiles_per_seq = seq // TM
    n_tiles = batch * tiles_per_seq
    final = fnorm is not None
    cast_next = next_f32 is not None
    smem = pl.BlockSpec(memory_space=pltpu.SMEM)
    hbm = pl.BlockSpec(memory_space=pl.ANY)

    def in_tile(s):
        t = jnp.minimum(s, n_tiles - 1)
        return (t // tiles_per_seq, t % tiles_per_seq, 0)

    def out_tile(s):
        t = jnp.maximum(s - 1, 0)
        return (t // tiles_per_seq, t % tiles_per_seq, 0)

    in_specs = [
        pl.BlockSpec((1, TM, D_MODEL), in_tile),
        _resident(n1.shape),
        _resident((D_MODEL, IN_WIDTH)),
        _resident((CONV_K, CONV_WIDTH), layer),
        smem,
        _resident((N_KV_HEADS, 2 * BLOCK, GROUP * BLOCK)),
        _resident((POOL_WIDTH, POOL_WIDTH), layer),
        _resident(pscale.shape),
        _resident((D_MODEL, D_MODEL)),
        _resident(n2.shape),
        hbm,
        hbm,
    ]
    args = [x, n1, win, convw, sinks, bias, wbd, pscale, wout, n2, w1, w2]
    if final:
        in_specs.append(_resident((1, D_MODEL)))
        args.append(fnorm)
    out_specs = [pl.BlockSpec((1, TM, D_MODEL), out_tile)]
    out_shape = [jax.ShapeDtypeStruct(x.shape, x.dtype)]
    cast_scratch = []
    if cast_next:
        in_specs += [hbm] * len(next_f32)
        args += list(next_f32)
        out_specs += [hbm] * len(next_f32)
        out_shape += [jax.ShapeDtypeStruct(w.shape[1:], _BF16) for w in next_f32]
        rows = [w.shape[1] // n_tiles for w in next_f32]
        assert all(r * n_tiles == w.shape[1] and r % BF16_SUBLANES == 0 for r, w in zip(rows, next_f32))
        cast_scratch = ([pltpu.VMEM((r, w.shape[2]), _F32) for r, w in zip(rows, next_f32)]
                        + [pltpu.VMEM((r, w.shape[2]), _BF16) for r, w in zip(rows, next_f32)]
                        + [pltpu.SemaphoreType.DMA((len(next_f32),))] * 2)
    scratch = [
        pltpu.VMEM((2, TM, D_MODEL), _F32),
        pltpu.VMEM((2, TM, D_MODEL), _BF16),
        pltpu.VMEM((TM, D_FF), _BF16),
        pltpu.VMEM(((TM // BLOCK) * N_KV_HEADS, KV_WIDTH, GROUP * BLOCK), _BF16),
        pltpu.VMEM((KV_HALO + TM, KV_WIDTH), _BF16),
        pltpu.VMEM((KV_HALO + TM, KV_WIDTH), _F32),
        pltpu.VMEM((TM, _GATES_WIDTH), _F32),
        pltpu.VMEM((CONV_HALO + TM, CONV_WIDTH), _F32),
        pltpu.VMEM((POOL_HALO + TM, POOL_WIDTH), _F32),
        pltpu.VMEM((POOL_HALO + TM, POOL_WIDTH), _F32),
        pltpu.VMEM((POOL_HALO + TM, POOL_WIDTH), _F32),
        pltpu.VMEM((POOL_HALO + TM, POOL_WIDTH), _F32),
        pltpu.VMEM((TM, D_MODEL), _BF16),
        pltpu.VMEM((D_MODEL, D_FF), _BF16),
        pltpu.VMEM((D_FF, D_MODEL), _BF16),
        pltpu.SemaphoreType.DMA((2,)),
    ]
    assert len(scratch) == _N_BASE_SCRATCH
    scratch += cast_scratch
    outs = pl.pallas_call(
        functools.partial(_layer_kernel, layer=layer, final=final, cast_next=cast_next,
                          tiles_per_seq=tiles_per_seq),
        grid=(n_tiles + 1,),
        in_specs=in_specs,
        out_specs=out_specs,
        out_shape=out_shape,
        scratch_shapes=scratch,
        compiler_params=pltpu.CompilerParams(
            dimension_semantics=("arbitrary",),
            vmem_limit_bytes=V7X_VMEM_LIMIT_BYTES,
        ),
        name="hybrid_layer_final" if final else "hybrid_layer",
    )(*args)
    return outs[0], outs[1:]


def kernel(x, norm1, w_in, conv_w, sinks, pool_w, pool_scale, w_out, norm2, w1, w2, rel_bias, final_norm):
    depth = w_in.shape[0]
    bkt = jnp.asarray(_bucket_table())
    bias, wbd = _tables_call(rel_bias, pool_w, bkt)
    weights_f32 = (w_in, w_out, w1, w2)
    win, wout, w1b, w2b = [w[0].astype(_BF16) for w in weights_f32]
    for l in range(depth):
        last = l == depth - 1
        fnorm = final_norm.reshape(1, D_MODEL) if last else None
        x, nxt = _layer_call(x, l, norm1, win, conv_w, sinks, bias, wbd, pool_scale, wout, norm2, w1b, w2b, fnorm,
                             None if last else weights_f32)
        if not last:
            win, wout, w1b, w2b = nxt
    return x
```

```python
import functools
import math

import numpy as np
import jax
import jax.numpy as jnp
from jax import lax
from jax.experimental import pallas as pl
from jax.experimental.pallas import tpu as pltpu

D_MODEL = 1024
HEAD_DIM = 64
ATTN_WIDTH = 512
CONV_WIDTH = 256
CONV_K = 3
POOL_WIDTH = 256
N_Q_HEADS = 8
N_KV_HEADS = 2
GROUP = N_Q_HEADS // N_KV_HEADS
KV_WIDTH = N_KV_HEADS * HEAD_DIM
POOL_WINDOWS = (2, 4, 8, 16)
POOL_GROUP = 64
IN_WIDTH = 1792
D_FF = 4096
WINDOW = 128
BLOCK = 128
N_BUCKETS = 32
MAX_DISTANCE = 128
EPS = 1e-6
NEG = -1e30
LOG2E = math.log2(math.e)
Q_SCALE = LOG2E / math.sqrt(HEAD_DIM)

_QKV_END = ATTN_WIDTH + 2 * KV_WIDTH
_GATES_WIDTH = IN_WIDTH - _QKV_END

TM = 512
KV_HALO = BLOCK
CONV_HALO = 8
POOL_HALO = 32
FF_PIECE = 512
BF16_SUBLANES = 16
LANES = 128
PREP_STEPS = 8
V7X_VMEM_LIMIT_BYTES = 60 * 1024 * 1024
_PHASE_ORDER = "m x m x m x m x m x m x m x m x x x x x m x x m m".split()

_F32 = jnp.float32
_BF16 = jnp.bfloat16


def _bucket_table():
    kj = np.arange(2 * BLOCK, dtype=np.int32)[:, None]
    qi = np.arange(BLOCK, dtype=np.int32)[None, :] + BLOCK
    dist = qi - kj
    n = np.maximum(dist, 0)
    max_exact = N_BUCKETS // 2
    nf = np.maximum(n, 1).astype(np.float32)
    large = max_exact + (np.log(nf / np.float32(max_exact)) / np.float32(math.log(MAX_DISTANCE / max_exact))
                         * np.float32(N_BUCKETS - max_exact)).astype(np.int32)
    large = np.minimum(large, N_BUCKETS - 1)
    bucket = np.where(n < max_exact, n, large)
    valid = (dist >= 0) & (dist < WINDOW)
    return np.where(valid, bucket, -1).astype(np.int32)


def _rmsnorm(x, g):
    ms = jnp.mean(x * x, axis=-1)
    r = lax.rsqrt(ms.reshape(-1, LANES) + EPS).reshape(-1, 1)
    return x * r * g


def _prep_kernel(relb_ref, poolw_ref, bkt_ref, *refs):
    n_w = (len(refs) - 2) // 2
    w_f32, (bias_ref, wbd_ref), w_bf16 = refs[:n_w], refs[n_w:n_w + 2], refs[n_w + 2:]
    for src, dst in zip(w_f32, w_bf16):
        dst[...] = src[...].astype(_BF16)

    @pl.when(pl.program_id(0) == 0)
    def _():
        _build_tables(relb_ref, poolw_ref, bkt_ref, bias_ref, wbd_ref)


def _build_tables(relb_ref, poolw_ref, bkt_ref, bias_ref, wbd_ref):
    bkt = bkt_ref[...]
    for h in range(N_Q_HEADS):
        acc = jnp.where(bkt < 0, NEG, 0.0).astype(_F32)
        for b in range(N_BUCKETS):
            acc = jnp.where(bkt == b, relb_ref[b, h] * LOG2E, acc)
        g, gi = divmod(h, GROUP)
        bias_ref[g, :, gi * BLOCK:(gi + 1) * BLOCK] = acc
    rows = lax.broadcasted_iota(jnp.int32, (POOL_GROUP, POOL_WIDTH), 0)
    cols = lax.broadcasted_iota(jnp.int32, (POOL_GROUP, POOL_WIDTH), 1)
    rep = jnp.where((cols % POOL_GROUP) == rows, 1.0, 0.0).astype(_BF16)
    for l in range(poolw_ref.shape[0]):
        for g in range(len(POOL_WINDOWS)):
            tiled = jnp.dot(poolw_ref[l, g].astype(_BF16), rep, preferred_element_type=_F32)
            wbd_ref[l, g * POOL_GROUP:(g + 1) * POOL_GROUP, :] = jnp.where(cols // POOL_GROUP == g, tiled,
                                                                          0.0).astype(_BF16)


def _prep_call(relb, poolw, bkt, weights_f32):
    depth = poolw.shape[0]

    def whole(a):
        return pl.BlockSpec(a.shape, lambda i: (0,) * len(a.shape))

    table_shapes = [jax.ShapeDtypeStruct((N_KV_HEADS, 2 * BLOCK, GROUP * BLOCK), _F32),
                    jax.ShapeDtypeStruct((depth, POOL_WIDTH, POOL_WIDTH), _BF16)]
    rows = [w.shape[1] // PREP_STEPS for w in weights_f32]
    assert all(r * PREP_STEPS == w.shape[1] and r % BF16_SUBLANES == 0 for r, w in zip(rows, weights_f32))
    outs = pl.pallas_call(
        _prep_kernel,
        grid=(PREP_STEPS,),
        in_specs=[pl.BlockSpec(memory_space=pltpu.SMEM), whole(poolw), whole(bkt)]
        + [pl.BlockSpec((None, r, w.shape[2]), lambda i: (0, i, 0)) for r, w in zip(rows, weights_f32)],
        out_specs=[whole(s) for s in table_shapes]
        + [pl.BlockSpec((r, w.shape[2]), lambda i: (i, 0)) for r, w in zip(rows, weights_f32)],
        out_shape=table_shapes + [jax.ShapeDtypeStruct(w.shape[1:], _BF16) for w in weights_f32],
        compiler_params=pltpu.CompilerParams(dimension_semantics=("arbitrary",),
                                             vmem_limit_bytes=V7X_VMEM_LIMIT_BYTES),
        name="hybrid_prep",
    )(relb, poolw, bkt, *weights_f32)
    return outs[0], outs[1], outs[2:]


def _attention_scores(n, g, first_tile, q_scr, k_scr, bias_scr, sink_row):
    r0 = n * BLOCK
    kband = k_scr[r0:r0 + 2 * BLOCK, :]
    s_t = jnp.dot(kband, q_scr[n * N_KV_HEADS + g], preferred_element_type=_F32) + bias_scr[g]
    if n == 0:
        s_t = jnp.concatenate([s_t[:BLOCK] + jnp.where(first_tile, NEG, 0.0), s_t[BLOCK:]], axis=0)
    m = jnp.maximum(jnp.max(s_t, axis=0, keepdims=True), sink_row)
    return jnp.exp2(s_t - m).astype(_BF16), jnp.exp2(sink_row - m)


def _attention_output(n, g, e, e_sink, v_scr, mixed_scr):
    r0 = n * BLOCK
    v_t = v_scr[r0:r0 + 2 * BLOCK, :].T[g * HEAD_DIM:(g + 1) * HEAD_DIM, :]
    v_aug = jnp.concatenate([v_t, jnp.ones((BF16_SUBLANES, 2 * BLOCK), _F32)], axis=0).astype(_BF16)
    o_aug = jnp.dot(v_aug, e, preferred_element_type=_F32)
    inv_denom = 1.0 / (o_aug[HEAD_DIM:HEAD_DIM + 1, :] + e_sink)
    o_t = o_aug[0:HEAD_DIM, :] * inv_denom
    o4 = jnp.concatenate([o_t[:, gi * BLOCK:(gi + 1) * BLOCK] for gi in range(GROUP)], axis=0)
    mixed_scr[r0:r0 + BLOCK, g * GROUP * HEAD_DIM:(g + 1) * GROUP * HEAD_DIM] = o4.T.astype(_BF16)


def _mlp_half(x1_ref, h2_ref, w1_ref, w2_ref, fnorm_ref, o_ref, act_scr):
    for lo in range(0, D_FF, FF_PIECE):
        hid = jnp.dot(h2_ref[...], w1_ref[:, lo:lo + FF_PIECE], preferred_element_type=_F32)
        act_scr[:, lo:lo + FF_PIECE] = jnp.square(jnp.maximum(hid, 0.0)).astype(_BF16)
        yield
    halfd = D_MODEL // 2
    parts = []
    for i in range(2):
        parts.append(jnp.dot(act_scr[...], w2_ref[:, i * halfd:(i + 1) * halfd], preferred_element_type=_F32))
        yield
    out = x1_ref[...] + jnp.concatenate(parts, axis=1)
    if fnorm_ref is not None:
        out = _rmsnorm(out, fnorm_ref[...])
    o_ref[0] = out
    yield


def _mixer_half(x_ref, x1_ref, h2_ref, seq_tile, n1_ref, win_ref, convw_ref, sinks_ref, pscale_ref, wout_ref,
                n2_ref, layer, q_scr, k_scr, v_scr, g_scr, u_scr, p_scr, a1_scr, a2_scr, a3_scr, mixed_scr,
                bias_scr, wbd_scr):
    first_tile = seq_tile == 0

    x = x_ref[0]
    h = _rmsnorm(x, n1_ref[...]).astype(_BF16)
    q = jnp.dot(h, win_ref[:, 0:ATTN_WIDTH], preferred_element_type=_F32) * Q_SCALE
    zeros = jnp.zeros((HEAD_DIM, GROUP * BLOCK), _F32)
    for n in range(TM // BLOCK):
        for g in range(N_KV_HEADS):
            qg_t = q[n * BLOCK:(n + 1) * BLOCK, g * GROUP * HEAD_DIM:(g + 1) * GROUP * HEAD_DIM].T
            qcat = jnp.concatenate([qg_t[gi * HEAD_DIM:(gi + 1) * HEAD_DIM, :] for gi in range(GROUP)], axis=1)
            rhs = jnp.concatenate([qcat, zeros] if g == 0 else [zeros, qcat], axis=0)
            q_scr[n * N_KV_HEADS + g] = rhs.astype(_BF16)
    for r in range(0, TM, TM // 2):
        kv = jnp.dot(h[r:r + TM // 2], win_ref[:, ATTN_WIDTH:_QKV_END], preferred_element_type=_F32)
        k_scr[KV_HALO + r:KV_HALO + r + TM // 2, :] = kv[:, 0:KV_WIDTH].astype(_BF16)
        v_scr[KV_HALO + r:KV_HALO + r + TM // 2, :] = kv[:, KV_WIDTH:2 * KV_WIDTH]
    yield

    lane = lax.broadcasted_iota(jnp.int32, (1, GROUP * BLOCK), 1)
    sink_rows = []
    for g in range(N_KV_HEADS):
        row = jnp.full((1, GROUP * BLOCK), sinks_ref[layer, g * GROUP + GROUP - 1], _F32)
        for gi in range(GROUP - 2, -1, -1):
            row = jnp.where(lane < (gi + 1) * BLOCK, sinks_ref[layer, g * GROUP + gi], row)
        sink_rows.append(row * LOG2E)
    gates_half = _GATES_WIDTH // 2
    n_pairs = (TM // BLOCK) * N_KV_HEADS
    pending = None
    for k in range(n_pairs):
        n, g = divmod(k, N_KV_HEADS)
        if pending is not None:
            _attention_output(*pending, v_scr, mixed_scr)
        pending = (n, g) + _attention_scores(n, g, first_tile, q_scr, k_scr, bias_scr, sink_rows[g])
        yield
        i = k - (n_pairs - 2)
        if i >= 0:
            lo = _QKV_END + i * gates_half
            g_scr[:, i * gates_half:(i + 1) * gates_half] = jnp.dot(h, win_ref[:, lo:lo + gates_half],
                                                                    preferred_element_type=_F32)
            yield
    _attention_output(*pending, v_scr, mixed_scr)
    k_scr[0:KV_HALO, :] = k_scr[TM:TM + KV_HALO, :]
    v_scr[0:KV_HALO, :] = v_scr[TM:TM + KV_HALO, :]

    c0 = CONV_HALO
    u_scr[c0:c0 + TM, :] = g_scr[:, CONV_WIDTH:2 * CONV_WIDTH] * g_scr[:, 2 * CONV_WIDTH:3 * CONV_WIDTH]
    cw = convw_ref[...]
    y = (cw[0:1, :] * u_scr[c0 - 2:c0 - 2 + TM, :] + cw[1:2, :] * u_scr[c0 - 1:c0 - 1 + TM, :]
         + cw[2:3, :] * u_scr[c0:c0 + TM, :])
    mixed_scr[:, ATTN_WIDTH:ATTN_WIDTH + CONV_WIDTH] = (g_scr[:, 0:CONV_WIDTH] * y).astype(_BF16)
    u_scr[0:CONV_HALO, :] = u_scr[TM:TM + CONV_HALO, :]

    p0 = POOL_HALO
    pe = p0 + TM
    p = g_scr[:, 3 * CONV_WIDTH:_GATES_WIDTH]
    p_scr[p0:pe, :] = p
    a1_scr[8:pe, :] = p_scr[8:pe, :] + p_scr[7:pe - 1, :]
    a2_scr[16:pe, :] = a1_scr[16:pe, :] + a1_scr[14:pe - 2, :]
    hi = slice(2 * POOL_GROUP, POOL_WIDTH)
    a3_scr[24:pe, hi] = a2_scr[24:pe, hi] + a2_scr[20:pe - 4, hi]
    s16 = a3_scr[p0:pe, hi] + a3_scr[p0 - 8:pe - 8, hi]
    first_group = lax.broadcasted_iota(jnp.int32, (TM, 2 * POOL_GROUP), 1) < POOL_GROUP
    ssum = jnp.concatenate([jnp.where(first_group, a1_scr[p0:pe, 0:2 * POOL_GROUP], a2_scr[p0:pe, 0:2 * POOL_GROUP]),
                            jnp.where(first_group, a3_scr[p0:pe, hi], s16)], axis=1)
    head = max(POOL_WINDOWS)
    hlane = lax.broadcasted_iota(jnp.int32, (head, POOL_WIDTH), 1)
    win = jnp.where(hlane < POOL_GROUP, POOL_WINDOWS[0],
                    jnp.where(hlane < 2 * POOL_GROUP, POOL_WINDOWS[1],
                              jnp.where(hlane < 3 * POOL_GROUP, POOL_WINDOWS[2], POOL_WINDOWS[3])))
    t = seq_tile * TM + lax.broadcasted_iota(jnp.int32, (head, POOL_WIDTH), 0)
    inv_head = 1.0 / jnp.minimum(t + 1, win).astype(_F32)
    inv_count = jnp.concatenate([inv_head, jnp.broadcast_to(inv_head[head - 1:head, :], (TM - head, POOL_WIDTH))],
                                axis=0)
    pooled = (ssum * inv_count - p).astype(_BF16)
    p_scr[0:POOL_HALO, :] = p_scr[TM:TM + POOL_HALO, :]
    yield
    for r in range(0, TM, TM // 2):
        mixed_pool = jnp.dot(pooled[r:r + TM // 2], wbd_scr[...], preferred_element_type=_F32) * pscale_ref[...]
        mixed_scr[r:r + TM // 2, ATTN_WIDTH + CONV_WIDTH:D_MODEL] = mixed_pool.astype(_BF16)
    yield

    x1 = x + jnp.dot(mixed_scr[...], wout_ref[...], preferred_element_type=_F32)
    x1_ref[...] = x1
    h2_ref[...] = _rmsnorm(x1, n2_ref[...]).astype(_BF16)
    yield


def _step_rows(s, buf):
    rows = buf.shape[0]
    return pl.ds(pl.multiple_of(s * rows, rows), rows)


def _fetch_copies(s, layer, srcs, in_bufs, sems):
    return [pltpu.make_async_copy(src.at[layer + 1, _step_rows(s, buf), :], buf, sems.at[i])
            for i, (src, buf) in enumerate(zip(srcs, in_bufs))]


def _put_copies(s, dsts, out_bufs, sems):
    return [pltpu.make_async_copy(buf, dst.at[_step_rows(s, buf), :], sems.at[i])
            for i, (dst, buf) in enumerate(zip(dsts, out_bufs))]


_N_BASE_SCRATCH = 16


def _layer_kernel(x_ref, n1_ref, win_ref, convw_ref, sinks_ref, bias_scr, wbd_scr, pscale_ref, wout_ref,
                  n2_ref, w1_hbm, w2_hbm, *rest, layer, final, cast_next, tiles_per_seq):
    rest = list(rest)
    fnorm_ref = rest.pop(0) if final else None
    next_f32 = [rest.pop(0) for _ in range(4)] if cast_next else []
    o_ref = rest.pop(0)
    next_bf16 = [rest.pop(0) for _ in range(4)] if cast_next else []
    (x1_scr, h2_scr, act_scr, q_scr, k_scr, v_scr, g_scr, u_scr, p_scr, a1_scr, a2_scr, a3_scr, mixed_scr,
     w1_ref, w2_ref, mlp_w_sems) = rest[:_N_BASE_SCRATCH]
    s = pl.program_id(0)
    n_tiles = pl.num_programs(0) - 1
    slot = s % 2
    seq_tile = jnp.minimum(s, n_tiles - 1) % tiles_per_seq

    mlp_w_copies = [pltpu.make_async_copy(src, dst, mlp_w_sems.at[i])
                    for i, (src, dst) in enumerate(((w1_hbm, w1_ref), (w2_hbm, w2_ref)))]

    @pl.when(s == 0)
    def _():
        for copy in mlp_w_copies:
            copy.start()

    @pl.when(s == 1)
    def _():
        for copy in mlp_w_copies:
            copy.wait()

    if cast_next:
        cast_scratch = rest[_N_BASE_SCRATCH:]
        in_bufs, out_bufs, (in_sems, out_sems) = cast_scratch[0:4], cast_scratch[4:8], cast_scratch[8:10]

        @pl.when(s < n_tiles)
        def _():
            for fetch in _fetch_copies(s, layer, next_f32, in_bufs, in_sems):
                fetch.start()

    @pl.when(jnp.logical_and(seq_tile == 0, s < n_tiles))
    def _():
        k_scr[0:KV_HALO, :] = jnp.zeros((KV_HALO, KV_WIDTH), _BF16)
        v_scr[0:KV_HALO, :] = jnp.zeros((KV_HALO, KV_WIDTH), _F32)
        u_scr[0:CONV_HALO, :] = jnp.zeros((CONV_HALO, CONV_WIDTH), _F32)
        p_scr[0:POOL_HALO, :] = jnp.zeros((POOL_HALO, POOL_WIDTH), _F32)

    def mlp_half():
        return _mlp_half(x1_scr.at[1 - slot], h2_scr.at[1 - slot], w1_ref, w2_ref, fnorm_ref if final else None,
                         o_ref, act_scr)

    def mixer_half():
        this_layer = pl.ds(layer, 1)
        return _mixer_half(x_ref, x1_scr.at[slot], h2_scr.at[slot], seq_tile, n1_ref.at[this_layer], win_ref,
                           convw_ref, sinks_ref, pscale_ref.at[this_layer], wout_ref, n2_ref.at[this_layer], layer,
                           q_scr, k_scr, v_scr, g_scr, u_scr, p_scr, a1_scr, a2_scr, a3_scr, mixed_scr,
                           bias_scr, wbd_scr)

    @pl.when(s == 0)
    def _():
        for _ in mixer_half():
            pass

    @pl.when(jnp.logical_and(s > 0, s < n_tiles))
    def _():
        mlp, mixer = mlp_half(), mixer_half()
        for who in _PHASE_ORDER:
            next(mlp if who == "m" else mixer)
        for half in (mlp, mixer):
            assert next(half, "done") == "done", "phase order does not cover every phase"

    @pl.when(s == n_tiles)
    def _():
        for _ in mlp_half():
            pass

    if cast_next:
        @pl.when(s < n_tiles)
        def _():
            @pl.when(s > 0)
            def _():
                for put in _put_copies(s - 1, next_bf16, out_bufs, out_sems):
                    put.wait()

            for fetch, put, ibuf, obuf in zip(_fetch_copies(s, layer, next_f32, in_bufs, in_sems),
                                              _put_copies(s, next_bf16, out_bufs, out_sems), in_bufs, out_bufs):
                fetch.wait()
                obuf[...] = ibuf[...].astype(_BF16)
                put.start()

        @pl.when(s == n_tiles)
        def _():
            for put in _put_copies(s - 1, next_bf16, out_bufs, out_sems):
                put.wait()


def _resident(shape, layer=None):
    if layer is None:
        return pl.BlockSpec(shape, lambda s: (0,) * len(shape), pipeline_mode=pl.Buffered(1))
    return pl.BlockSpec((None,) + shape, lambda s: (layer,) + (0,) * len(shape), pipeline_mode=pl.Buffered(1))


def _layer_call(x, layer, n1, win, convw, sinks, bias, wbd, pscale, wout, n2, w1, w2, fnorm, next_f32):
    batch, seq, _ = x.shape
    assert seq % TM == 0 and TM % BLOCK == 0
    tiles_per_seq = seq // TM
    n_tiles = batch * tiles_per_seq
    final = fnorm is not None
    cast_next = next_f32 is not None
    smem = pl.BlockSpec(memory_space=pltpu.SMEM)
    hbm = pl.BlockSpec(memory_space=pl.ANY)

    def in_tile(s):
        t = jnp.minimum(s, n_tiles - 1)
        return (t // tiles_per_seq, t % tiles_per_seq, 0)

    def out_tile(s):
        t = jnp.maximum(s - 1, 0)
        return (t // tiles_per_seq, t % tiles_per_seq, 0)

    in_specs = [
        pl.BlockSpec((1, TM, D_MODEL), in_tile),
        _resident(n1.shape),
        _resident((D_MODEL, IN_WIDTH)),
        _resident((CONV_K, CONV_WIDTH), layer),
        smem,
        _resident((N_KV_HEADS, 2 * BLOCK, GROUP * BLOCK)),
        _resident((POOL_WIDTH, POOL_WIDTH), layer),
        _resident(pscale.shape),
        _resident((D_MODEL, D_MODEL)),
        _resident(n2.shape),
        hbm,
        hbm,
    ]
    args = [x, n1, win, convw, sinks, bias, wbd, pscale, wout, n2, w1, w2]
    if final:
        in_specs.append(_resident((1, D_MODEL)))
        args.append(fnorm)
    out_specs = [pl.BlockSpec((1, TM, D_MODEL), out_tile)]
    out_shape = [jax.ShapeDtypeStruct(x.shape, x.dtype)]
    cast_scratch = []
    if cast_next:
        in_specs += [hbm] * len(next_f32)
        args += list(next_f32)
        out_specs += [hbm] * len(next_f32)
        out_shape += [jax.ShapeDtypeStruct(w.shape[1:], _BF16) for w in next_f32]
        rows = [w.shape[1] // n_tiles for w in next_f32]
        assert all(r * n_tiles == w.shape[1] and r % BF16_SUBLANES == 0 for r, w in zip(rows, next_f32))
        cast_scratch = ([pltpu.VMEM((r, w.shape[2]), _F32) for r, w in zip(rows, next_f32)]
                        + [pltpu.VMEM((r, w.shape[2]), _BF16) for r, w in zip(rows, next_f32)]
                        + [pltpu.SemaphoreType.DMA((len(next_f32),))] * 2)
    scratch = [
        pltpu.VMEM((2, TM, D_MODEL), _F32),
        pltpu.VMEM((2, TM, D_MODEL), _BF16),
        pltpu.VMEM((TM, D_FF), _BF16),
        pltpu.VMEM(((TM // BLOCK) * N_KV_HEADS, KV_WIDTH, GROUP * BLOCK), _BF16),
        pltpu.VMEM((KV_HALO + TM, KV_WIDTH), _BF16),
        pltpu.VMEM((KV_HALO + TM, KV_WIDTH), _F32),
        pltpu.VMEM((TM, _GATES_WIDTH), _F32),
        pltpu.VMEM((CONV_HALO + TM, CONV_WIDTH), _F32),
        pltpu.VMEM((POOL_HALO + TM, POOL_WIDTH), _F32),
        pltpu.VMEM((POOL_HALO + TM, POOL_WIDTH), _F32),
        pltpu.VMEM((POOL_HALO + TM, POOL_WIDTH), _F32),
        pltpu.VMEM((POOL_HALO + TM, POOL_WIDTH), _F32),
        pltpu.VMEM((TM, D_MODEL), _BF16),
        pltpu.VMEM((D_MODEL, D_FF), _BF16),
        pltpu.VMEM((D_FF, D_MODEL), _BF16),
        pltpu.SemaphoreType.DMA((2,)),
    ]
    assert len(scratch) == _N_BASE_SCRATCH
    scratch += cast_scratch
    outs = pl.pallas_call(
        functools.partial(_layer_kernel, layer=layer, final=final, cast_next=cast_next,
                          tiles_per_seq=tiles_per_seq),
        grid=(n_tiles + 1,),
        in_specs=in_specs,
        out_specs=out_specs,
        out_shape=out_shape,
        scratch_shapes=scratch,
        compiler_params=pltpu.CompilerParams(
            dimension_semantics=("arbitrary",),
            vmem_limit_bytes=V7X_VMEM_LIMIT_BYTES,
        ),
        name="hybrid_layer_final" if final else "hybrid_layer",
    )(*args)
    return outs[0], outs[1:]


def kernel(x, norm1, w_in, conv_w, sinks, pool_w, pool_scale, w_out, norm2, w1, w2, rel_bias, final_norm):
    depth = w_in.shape[0]
    bkt = jnp.asarray(_bucket_table())
    weights_f32 = (w_in, w_out, w1, w2)
    bias, wbd, (win, wout, w1b, w2b) = _prep_call(rel_bias, pool_w, bkt, weights_f32)
    for l in range(depth):
        last = l == depth - 1
        fnorm = final_norm.reshape(1, D_MODEL) if last else None
        x, nxt = _layer_call(x, l, norm1, win, conv_w, sinks, bias, wbd, pool_scale, wout, norm2, w1b, w2b, fnorm,
                             None if last else weights_f32)
        if not last:
            win, wout, w1b, w2b = nxt
    return x
```

```python
import functools
import math

import numpy as np
import jax
import jax.numpy as jnp
from jax import lax
from jax.experimental import pallas as pl
from jax.experimental.pallas import tpu as pltpu

D_MODEL = 1024
HEAD_DIM = 64
ATTN_WIDTH = 512
CONV_WIDTH = 256
CONV_K = 3
POOL_WIDTH = 256
N_Q_HEADS = 8
N_KV_HEADS = 2
GROUP = N_Q_HEADS // N_KV_HEADS
KV_WIDTH = N_KV_HEADS * HEAD_DIM
POOL_WINDOWS = (2, 4, 8, 16)
POOL_GROUP = 64
IN_WIDTH = 1792
D_FF = 4096
WINDOW = 128
BLOCK = 128
N_BUCKETS = 32
MAX_DISTANCE = 128
EPS = 1e-6
NEG = -1e30
LOG2E = math.log2(math.e)
Q_SCALE = LOG2E / math.sqrt(HEAD_DIM)

_QKV_END = ATTN_WIDTH + 2 * KV_WIDTH
_GATES_WIDTH = IN_WIDTH - _QKV_END

TM = 512
KV_HALO = BLOCK
CONV_HALO = 8
POOL_HALO = 32
FF_PIECE = 512
BF16_SUBLANES = 16
LANES = 128
PREP_STEPS = 8
V7X_VMEM_LIMIT_BYTES = 60 * 1024 * 1024
_PHASE_ORDER = "m x m x m x m x m x m x m x m x x x x x m x x m m".split()

_F32 = jnp.float32
_BF16 = jnp.bfloat16


def _bucket_table():
    kj = np.arange(2 * BLOCK, dtype=np.int32)[:, None]
    qi = np.arange(BLOCK, dtype=np.int32)[None, :] + BLOCK
    dist = qi - kj
    n = np.maximum(dist, 0)
    max_exact = N_BUCKETS // 2
    nf = np.maximum(n, 1).astype(np.float32)
    large = max_exact + (np.log(nf / np.float32(max_exact)) / np.float32(math.log(MAX_DISTANCE / max_exact))
                         * np.float32(N_BUCKETS - max_exact)).astype(np.int32)
    large = np.minimum(large, N_BUCKETS - 1)
    bucket = np.where(n < max_exact, n, large)
    valid = (dist >= 0) & (dist < WINDOW)
    return np.where(valid, bucket, -1).astype(np.int32)


def _rmsnorm(x, g):
    ms = jnp.mean(x * x, axis=-1)
    r = lax.rsqrt(ms.reshape(-1, LANES) + EPS).reshape(-1, 1)
    return x * r * g


def _prep_kernel(relb_ref, poolw_ref, bkt_ref, *refs):
    n_w = (len(refs) - 2) // 2
    w_f32, (bias_ref, wbd_ref), w_bf16 = refs[:n_w], refs[n_w:n_w + 2], refs[n_w + 2:]
    for src, dst in zip(w_f32, w_bf16):
        dst[...] = src[...].astype(_BF16)

    @pl.when(pl.program_id(0) == 0)
    def _():
        _build_tables(relb_ref, poolw_ref, bkt_ref, bias_ref, wbd_ref)


def _build_tables(relb_ref, poolw_ref, bkt_ref, bias_ref, wbd_ref):
    bkt = bkt_ref[...]
    for h in range(N_Q_HEADS):
        acc = jnp.where(bkt < 0, NEG, 0.0).astype(_F32)
        for b in range(N_BUCKETS):
            acc = jnp.where(bkt == b, relb_ref[b, h] * LOG2E, acc)
        g, gi = divmod(h, GROUP)
        bias_ref[g, :, gi * BLOCK:(gi + 1) * BLOCK] = acc
    rows = lax.broadcasted_iota(jnp.int32, (POOL_GROUP, POOL_WIDTH), 0)
    cols = lax.broadcasted_iota(jnp.int32, (POOL_GROUP, POOL_WIDTH), 1)
    rep = jnp.where((cols % POOL_GROUP) == rows, 1.0, 0.0).astype(_BF16)
    for l in range(poolw_ref.shape[0]):
        for g in range(len(POOL_WINDOWS)):
            tiled = jnp.dot(poolw_ref[l, g].astype(_BF16), rep, preferred_element_type=_F32)
            wbd_ref[l, g * POOL_GROUP:(g + 1) * POOL_GROUP, :] = jnp.where(cols // POOL_GROUP == g, tiled,
                                                                          0.0).astype(_BF16)


def _prep_call(relb, poolw, bkt, weights_f32):
    depth = poolw.shape[0]

    def whole(a):
        return pl.BlockSpec(a.shape, lambda i: (0,) * len(a.shape))

    table_shapes = [jax.ShapeDtypeStruct((N_KV_HEADS, 2 * BLOCK, GROUP * BLOCK), _F32),
                    jax.ShapeDtypeStruct((depth, POOL_WIDTH, POOL_WIDTH), _BF16)]
    rows = [w.shape[1] // PREP_STEPS for w in weights_f32]
    assert all(r * PREP_STEPS == w.shape[1] and r % BF16_SUBLANES == 0 for r, w in zip(rows, weights_f32))
    outs = pl.pallas_call(
        _prep_kernel,
        grid=(PREP_STEPS,),
        in_specs=[pl.BlockSpec(memory_space=pltpu.SMEM), whole(poolw), whole(bkt)]
        + [pl.BlockSpec((None, r, w.shape[2]), lambda i: (0, i, 0)) for r, w in zip(rows, weights_f32)],
        out_specs=[whole(s) for s in table_shapes]
        + [pl.BlockSpec((r, w.shape[2]), lambda i: (i, 0)) for r, w in zip(rows, weights_f32)],
        out_shape=table_shapes + [jax.ShapeDtypeStruct(w.shape[1:], _BF16) for w in weights_f32],
        compiler_params=pltpu.CompilerParams(dimension_semantics=("arbitrary",),
                                             vmem_limit_bytes=V7X_VMEM_LIMIT_BYTES),
        name="hybrid_prep",
    )(relb, poolw, bkt, *weights_f32)
    return outs[0], outs[1], outs[2:]


def _attention_scores(n, g, first_tile, q_scr, k_scr, bias_scr, sink_row):
    r0 = n * BLOCK
    kband = k_scr[r0:r0 + 2 * BLOCK, :]
    s_t = jnp.dot(kband, q_scr[n * N_KV_HEADS + g], preferred_element_type=_F32) + bias_scr[g]
    if n == 0:
        s_t = jnp.concatenate([s_t[:BLOCK] + jnp.where(first_tile, NEG, 0.0), s_t[BLOCK:]], axis=0)
    m = jnp.maximum(jnp.max(s_t, axis=0, keepdims=True), sink_row)
    return jnp.exp2(s_t - m).astype(_BF16), jnp.exp2(sink_row - m)


def _attention_output(n, g, e, e_sink, v_scr, mixed_scr):
    r0 = n * BLOCK
    v_t = v_scr[r0:r0 + 2 * BLOCK, :].T[g * HEAD_DIM:(g + 1) * HEAD_DIM, :]
    v_aug = jnp.concatenate([v_t, jnp.ones((BF16_SUBLANES, 2 * BLOCK), _F32)], axis=0).astype(_BF16)
    o_aug = jnp.dot(v_aug, e, preferred_element_type=_F32)
    inv_denom = 1.0 / (o_aug[HEAD_DIM:HEAD_DIM + 1, :] + e_sink)
    o_t = o_aug[0:HEAD_DIM, :] * inv_denom
    o4 = jnp.concatenate([o_t[:, gi * BLOCK:(gi + 1) * BLOCK] for gi in range(GROUP)], axis=0)
    mixed_scr[r0:r0 + BLOCK, g * GROUP * HEAD_DIM:(g + 1) * GROUP * HEAD_DIM] = o4.T.astype(_BF16)


def _mlp_half(x1_ref, h2_ref, w1_ref, w2_ref, fnorm_ref, o_ref, act_scr):
    for lo in range(0, D_FF, FF_PIECE):
        hid = jnp.dot(h2_ref[...], w1_ref[:, lo:lo + FF_PIECE], preferred_element_type=_F32)
        act_scr[:, lo:lo + FF_PIECE] = jnp.square(jnp.maximum(hid, 0.0)).astype(_BF16)
        yield
    halfd = D_MODEL // 2
    for i in range(2):
        cols = slice(i * halfd, (i + 1) * halfd)
        o_ref[0, :, cols] = x1_ref[:, cols] + jnp.dot(act_scr[...], w2_ref[:, cols], preferred_element_type=_F32)
        yield
    if fnorm_ref is not None:
        o_ref[0] = _rmsnorm(o_ref[0], fnorm_ref[...])
    yield


def _mixer_half(x_ref, x1_ref, h2_ref, seq_tile, n1_ref, win_ref, convw_ref, sinks_ref, pscale_ref, wout_ref,
                n2_ref, layer, q_scr, k_scr, v_scr, g_scr, u_scr, p_scr, a1_scr, a2_scr, a3_scr, mixed_scr,
                bias_scr, wbd_scr):
    first_tile = seq_tile == 0

    x = x_ref[0]
    h = _rmsnorm(x, n1_ref[...]).astype(_BF16)
    q = jnp.dot(h, win_ref[:, 0:ATTN_WIDTH], preferred_element_type=_F32) * Q_SCALE
    zeros = jnp.zeros((HEAD_DIM, GROUP * BLOCK), _F32)
    for n in range(TM // BLOCK):
        for g in range(N_KV_HEADS):
            qg_t = q[n * BLOCK:(n + 1) * BLOCK, g * GROUP * HEAD_DIM:(g + 1) * GROUP * HEAD_DIM].T
            qcat = jnp.concatenate([qg_t[gi * HEAD_DIM:(gi + 1) * HEAD_DIM, :] for gi in range(GROUP)], axis=1)
            rhs = jnp.concatenate([qcat, zeros] if g == 0 else [zeros, qcat], axis=0)
            q_scr[n * N_KV_HEADS + g] = rhs.astype(_BF16)
    for r in range(0, TM, TM // 2):
        kv = jnp.dot(h[r:r + TM // 2], win_ref[:, ATTN_WIDTH:_QKV_END], preferred_element_type=_F32)
        k_scr[KV_HALO + r:KV_HALO + r + TM // 2, :] = kv[:, 0:KV_WIDTH].astype(_BF16)
        v_scr[KV_HALO + r:KV_HALO + r + TM // 2, :] = kv[:, KV_WIDTH:2 * KV_WIDTH]
    yield

    lane = lax.broadcasted_iota(jnp.int32, (1, GROUP * BLOCK), 1)
    sink_rows = []
    for g in range(N_KV_HEADS):
        row = jnp.full((1, GROUP * BLOCK), sinks_ref[layer, g * GROUP + GROUP - 1], _F32)
        for gi in range(GROUP - 2, -1, -1):
            row = jnp.where(lane < (gi + 1) * BLOCK, sinks_ref[layer, g * GROUP + gi], row)
        sink_rows.append(row * LOG2E)
    gates_half = _GATES_WIDTH // 2
    n_pairs = (TM // BLOCK) * N_KV_HEADS
    pending = None
    for k in range(n_pairs):
        n, g = divmod(k, N_KV_HEADS)
        if pending is not None:
            _attention_output(*pending, v_scr, mixed_scr)
        pending = (n, g) + _attention_scores(n, g, first_tile, q_scr, k_scr, bias_scr, sink_rows[g])
        yield
        i = k - (n_pairs - 2)
        if i >= 0:
            lo = _QKV_END + i * gates_half
            g_scr[:, i * gates_half:(i + 1) * gates_half] = jnp.dot(h, win_ref[:, lo:lo + gates_half],
                                                                    preferred_element_type=_F32)
            yield
    _attention_output(*pending, v_scr, mixed_scr)
    k_scr[0:KV_HALO, :] = k_scr[TM:TM + KV_HALO, :]
    v_scr[0:KV_HALO, :] = v_scr[TM:TM + KV_HALO, :]

    c0 = CONV_HALO
    u_scr[c0:c0 + TM, :] = g_scr[:, CONV_WIDTH:2 * CONV_WIDTH] * g_scr[:, 2 * CONV_WIDTH:3 * CONV_WIDTH]
    cw = convw_ref[...]
    y = (cw[0:1, :] * u_scr[c0 - 2:c0 - 2 + TM, :] + cw[1:2, :] * u_scr[c0 - 1:c0 - 1 + TM, :]
         + cw[2:3, :] * u_scr[c0:c0 + TM, :])
    mixed_scr[:, ATTN_WIDTH:ATTN_WIDTH + CONV_WIDTH] = (g_scr[:, 0:CONV_WIDTH] * y).astype(_BF16)
    u_scr[0:CONV_HALO, :] = u_scr[TM:TM + CONV_HALO, :]

    p0 = POOL_HALO
    pe = p0 + TM
    p = g_scr[:, 3 * CONV_WIDTH:_GATES_WIDTH]
    p_scr[p0:pe, :] = p
    a1_scr[8:pe, :] = p_scr[8:pe, :] + p_scr[7:pe - 1, :]
    a2_scr[16:pe, :] = a1_scr[16:pe, :] + a1_scr[14:pe - 2, :]
    hi = slice(2 * POOL_GROUP, POOL_WIDTH)
    a3_scr[24:pe, hi] = a2_scr[24:pe, hi] + a2_scr[20:pe - 4, hi]
    s16 = a3_scr[p0:pe, hi] + a3_scr[p0 - 8:pe - 8, hi]
    first_group = lax.broadcasted_iota(jnp.int32, (TM, 2 * POOL_GROUP), 1) < POOL_GROUP
    ssum = jnp.concatenate([jnp.where(first_group, a1_scr[p0:pe, 0:2 * POOL_GROUP], a2_scr[p0:pe, 0:2 * POOL_GROUP]),
                            jnp.where(first_group, a3_scr[p0:pe, hi], s16)], axis=1)
    head = max(POOL_WINDOWS)
    hlane = lax.broadcasted_iota(jnp.int32, (head, POOL_WIDTH), 1)
    win = jnp.where(hlane < POOL_GROUP, POOL_WINDOWS[0],
                    jnp.where(hlane < 2 * POOL_GROUP, POOL_WINDOWS[1],
                              jnp.where(hlane < 3 * POOL_GROUP, POOL_WINDOWS[2], POOL_WINDOWS[3])))
    t = seq_tile * TM + lax.broadcasted_iota(jnp.int32, (head, POOL_WIDTH), 0)
    inv_head = 1.0 / jnp.minimum(t + 1, win).astype(_F32)
    inv_count = jnp.concatenate([inv_head, jnp.broadcast_to(inv_head[head - 1:head, :], (TM - head, POOL_WIDTH))],
                                axis=0)
    pooled = (ssum * inv_count - p).astype(_BF16)
    p_scr[0:POOL_HALO, :] = p_scr[TM:TM + POOL_HALO, :]
    yield
    for r in range(0, TM, TM // 2):
        mixed_pool = jnp.dot(pooled[r:r + TM // 2], wbd_scr[...], preferred_element_type=_F32) * pscale_ref[...]
        mixed_scr[r:r + TM // 2, ATTN_WIDTH + CONV_WIDTH:D_MODEL] = mixed_pool.astype(_BF16)
    yield

    x1 = x + jnp.dot(mixed_scr[...], wout_ref[...], preferred_element_type=_F32)
    x1_ref[...] = x1
    h2_ref[...] = _rmsnorm(x1, n2_ref[...]).astype(_BF16)
    yield


def _step_rows(s, buf):
    rows = buf.shape[0]
    return pl.ds(pl.multiple_of(s * rows, rows), rows)


def _fetch_copies(s, layer, srcs, in_bufs, sems):
    return [pltpu.make_async_copy(src.at[layer + 1, _step_rows(s, buf), :], buf, sems.at[i])
            for i, (src, buf) in enumerate(zip(srcs, in_bufs))]


def _put_copies(s, dsts, out_bufs, sems):
    return [pltpu.make_async_copy(buf, dst.at[_step_rows(s, buf), :], sems.at[i])
            for i, (dst, buf) in enumerate(zip(dsts, out_bufs))]


_N_BASE_SCRATCH = 16


def _layer_kernel(x_ref, n1_ref, win_ref, convw_ref, sinks_ref, bias_scr, wbd_scr, pscale_ref, wout_ref,
                  n2_ref, w1_hbm, w2_hbm, *rest, layer, final, cast_next, tiles_per_seq):
    rest = list(rest)
    fnorm_ref = rest.pop(0) if final else None
    next_f32 = [rest.pop(0) for _ in range(4)] if cast_next else []
    o_ref = rest.pop(0)
    next_bf16 = [rest.pop(0) for _ in range(4)] if cast_next else []
    (x1_scr, h2_scr, act_scr, q_scr, k_scr, v_scr, g_scr, u_scr, p_scr, a1_scr, a2_scr, a3_scr, mixed_scr,
     w1_ref, w2_ref, mlp_w_sems) = rest[:_N_BASE_SCRATCH]
    s = pl.program_id(0)
    n_tiles = pl.num_programs(0) - 1
    slot = s % 2
    seq_tile = jnp.minimum(s, n_tiles - 1) % tiles_per_seq

    mlp_w_copies = [pltpu.make_async_copy(src, dst, mlp_w_sems.at[i])
                    for i, (src, dst) in enumerate(((w1_hbm, w1_ref), (w2_hbm, w2_ref)))]

    @pl.when(s == 0)
    def _():
        for copy in mlp_w_copies:
            copy.start()

    @pl.when(s == 1)
    def _():
        for copy in mlp_w_copies:
            copy.wait()

    if cast_next:
        cast_scratch = rest[_N_BASE_SCRATCH:]
        in_bufs, out_bufs, (in_sems, out_sems) = cast_scratch[0:4], cast_scratch[4:8], cast_scratch[8:10]

        @pl.when(s < n_tiles)
        def _():
            for fetch in _fetch_copies(s, layer, next_f32, in_bufs, in_sems):
                fetch.start()

    @pl.when(jnp.logical_and(seq_tile == 0, s < n_tiles))
    def _():
        k_scr[0:KV_HALO, :] = jnp.zeros((KV_HALO, KV_WIDTH), _BF16)
        v_scr[0:KV_HALO, :] = jnp.zeros((KV_HALO, KV_WIDTH), _F32)
        u_scr[0:CONV_HALO, :] = jnp.zeros((CONV_HALO, CONV_WIDTH), _F32)
        p_scr[0:POOL_HALO, :] = jnp.zeros((POOL_HALO, POOL_WIDTH), _F32)

    def mlp_half():
        return _mlp_half(x1_scr.at[1 - slot], h2_scr.at[1 - slot], w1_ref, w2_ref, fnorm_ref if final else None,
                         o_ref, act_scr)

    def mixer_half():
        this_layer = pl.ds(layer, 1)
        return _mixer_half(x_ref, x1_scr.at[slot], h2_scr.at[slot], seq_tile, n1_ref.at[this_layer], win_ref,
                           convw_ref, sinks_ref, pscale_ref.at[this_layer], wout_ref, n2_ref.at[this_layer], layer,
                           q_scr, k_scr, v_scr, g_scr, u_scr, p_scr, a1_scr, a2_scr, a3_scr, mixed_scr,
                           bias_scr, wbd_scr)

    @pl.when(s == 0)
    def _():
        for _ in mixer_half():
            pass

    @pl.when(jnp.logical_and(s > 0, s < n_tiles))
    def _():
        mlp, mixer = mlp_half(), mixer_half()
        for who in _PHASE_ORDER:
            next(mlp if who == "m" else mixer)
        for half in (mlp, mixer):
            assert next(half, "done") == "done", "phase order does not cover every phase"

    @pl.when(s == n_tiles)
    def _():
        for _ in mlp_half():
            pass

    if cast_next:
        @pl.when(s < n_tiles)
        def _():
            @pl.when(s > 0)
            def _():
                for put in _put_copies(s - 1, next_bf16, out_bufs, out_sems):
                    put.wait()

            for fetch, put, ibuf, obuf in zip(_fetch_copies(s, layer, next_f32, in_bufs, in_sems),
                                              _put_copies(s, next_bf16, out_bufs, out_sems), in_bufs, out_bufs):
                fetch.wait()
                obuf[...] = ibuf[...].astype(_BF16)
                put.start()

        @pl.when(s == n_tiles)
        def _():
            for put in _put_copies(s - 1, next_bf16, out_bufs, out_sems):
                put.wait()


def _resident(shape, layer=None):
    if layer is None:
        return pl.BlockSpec(shape, lambda s: (0,) * len(shape), pipeline_mode=pl.Buffered(1))
    return pl.BlockSpec((None,) + shape, lambda s: (layer,) + (0,) * len(shape), pipeline_mode=pl.Buffered(1))


def _layer_call(x, layer, n1, win, convw, sinks, bias, wbd, pscale, wout, n2, w1, w2, fnorm, next_f32):
    batch, seq, _ = x.shape
    assert seq % TM == 0 and TM % BLOCK == 0
    tiles_per_seq = seq // TM
    n_tiles = batch * tiles_per_seq
    final = fnorm is not None
    cast_next = next_f32 is not None
    smem = pl.BlockSpec(memory_space=pltpu.SMEM)
    hbm = pl.BlockSpec(memory_space=pl.ANY)

    def in_tile(s):
        t = jnp.minimum(s, n_tiles - 1)
        return (t // tiles_per_seq, t % tiles_per_seq, 0)

    def out_tile(s):
        t = jnp.maximum(s - 1, 0)
        return (t // tiles_per_seq, t % tiles_per_seq, 0)

    in_specs = [
        pl.BlockSpec((1, TM, D_MODEL), in_tile),
        _resident(n1.shape),
        _resident((D_MODEL, IN_WIDTH)),
        _resident((CONV_K, CONV_WIDTH), layer),
        smem,
        _resident((N_KV_HEADS, 2 * BLOCK, GROUP * BLOCK)),
        _resident((POOL_WIDTH, POOL_WIDTH), layer),
        _resident(pscale.shape),
        _resident((D_MODEL, D_MODEL)),
        _resident(n2.shape),
        hbm,
        hbm,
    ]
    args = [x, n1, win, convw, sinks, bias, wbd, pscale, wout, n2, w1, w2]
    if final:
        in_specs.append(_resident((1, D_MODEL)))
        args.append(fnorm)
    out_specs = [pl.BlockSpec((1, TM, D_MODEL), out_tile)]
    out_shape = [jax.ShapeDtypeStruct(x.shape, x.dtype)]
    cast_scratch = []
    if cast_next:
        in_specs += [hbm] * len(next_f32)
        args += list(next_f32)
        out_specs += [hbm] * len(next_f32)
        out_shape += [jax.ShapeDtypeStruct(w.shape[1:], _BF16) for w in next_f32]
        rows = [w.shape[1] // n_tiles for w in next_f32]
        assert all(r * n_tiles == w.shape[1] and r % BF16_SUBLANES == 0 for r, w in zip(rows, next_f32))
        cast_scratch = ([pltpu.VMEM((r, w.shape[2]), _F32) for r, w in zip(rows, next_f32)]
                        + [pltpu.VMEM((r, w.shape[2]), _BF16) for r, w in zip(rows, next_f32)]
                        + [pltpu.SemaphoreType.DMA((len(next_f32),))] * 2)
    scratch = [
        pltpu.VMEM((2, TM, D_MODEL), _F32),
        pltpu.VMEM((2, TM, D_MODEL), _BF16),
        pltpu.VMEM((TM, D_FF), _BF16),
        pltpu.VMEM(((TM // BLOCK) * N_KV_HEADS, KV_WIDTH, GROUP * BLOCK), _BF16),
        pltpu.VMEM((KV_HALO + TM, KV_WIDTH), _BF16),
        pltpu.VMEM((KV_HALO + TM, KV_WIDTH), _F32),
        pltpu.VMEM((TM, _GATES_WIDTH), _F32),
        pltpu.VMEM((CONV_HALO + TM, CONV_WIDTH), _F32),
        pltpu.VMEM((POOL_HALO + TM, POOL_WIDTH), _F32),
        pltpu.VMEM((POOL_HALO + TM, POOL_WIDTH), _F32),
        pltpu.VMEM((POOL_HALO + TM, POOL_WIDTH), _F32),
        pltpu.VMEM((POOL_HALO + TM, POOL_WIDTH), _F32),
        pltpu.VMEM((TM, D_MODEL), _BF16),
        pltpu.VMEM((D_MODEL, D_FF), _BF16),
        pltpu.VMEM((D_FF, D_MODEL), _BF16),
        pltpu.SemaphoreType.DMA((2,)),
    ]
    assert len(scratch) == _N_BASE_SCRATCH
    scratch += cast_scratch
    outs = pl.pallas_call(
        functools.partial(_layer_kernel, layer=layer, final=final, cast_next=cast_next,
                          tiles_per_seq=tiles_per_seq),
        grid=(n_tiles + 1,),
        in_specs=in_specs,
        out_specs=out_specs,
        out_shape=out_shape,
        scratch_shapes=scratch,
        compiler_params=pltpu.CompilerParams(
            dimension_semantics=("arbitrary",),
            vmem_limit_bytes=V7X_VMEM_LIMIT_BYTES,
        ),
        name="hybrid_layer_final" if final else "hybrid_layer",
    )(*args)
    return outs[0], outs[1:]


def kernel(x, norm1, w_in, conv_w, sinks, pool_w, pool_scale, w_out, norm2, w1, w2, rel_bias, final_norm):
    depth = w_in.shape[0]
    bkt = jnp.asarray(_bucket_table())
    weights_f32 = (w_in, w_out, w1, w2)
    bias, wbd, (win, wout, w1b, w2b) = _prep_call(rel_bias, pool_w, bkt, weights_f32)
    for l in range(depth):
        last = l == depth - 1
        fnorm = final_norm.reshape(1, D_MODEL) if last else None
        x, nxt = _layer_call(x, l, norm1, win, conv_w, sinks, bias, wbd, pool_scale, wout, norm2, w1b, w2b, fnorm,
                             None if last else weights_f32)
        if not last:
            win, wout, w1b, w2b = nxt
    return x
```

```python
import functools
import math

import numpy as np
import jax
import jax.numpy as jnp
from jax import lax
from jax.experimental import pallas as pl
from jax.experimental.pallas import tpu as pltpu

D_MODEL = 1024
HEAD_DIM = 64
ATTN_WIDTH = 512
CONV_WIDTH = 256
CONV_K = 3
POOL_WIDTH = 256
N_Q_HEADS = 8
N_KV_HEADS = 2
GROUP = N_Q_HEADS // N_KV_HEADS
KV_WIDTH = N_KV_HEADS * HEAD_DIM
POOL_WINDOWS = (2, 4, 8, 16)
POOL_GROUP = 64
IN_WIDTH = 1792
D_FF = 4096
WINDOW = 128
BLOCK = 128
N_BUCKETS = 32
MAX_DISTANCE = 128
EPS = 1e-6
NEG = -1e30
LOG2E = math.log2(math.e)
Q_SCALE = LOG2E / math.sqrt(HEAD_DIM)

_QKV_END = ATTN_WIDTH + 2 * KV_WIDTH
_GATES_WIDTH = IN_WIDTH - _QKV_END

TM = 512
KV_HALO = BLOCK
CONV_HALO = 8
POOL_HALO = 32
FF_PIECE = 512
BF16_SUBLANES = 16
LANES = 128
PREP_STEPS = 8
V7X_VMEM_LIMIT_BYTES = 60 * 1024 * 1024
_PHASE_ORDER = "m x m x m x m x m x m x m x m x x x x x m x x m m".split()

_F32 = jnp.float32
_BF16 = jnp.bfloat16


def _bucket_table():
    kj = np.arange(2 * BLOCK, dtype=np.int32)[:, None]
    qi = np.arange(BLOCK, dtype=np.int32)[None, :] + BLOCK
    dist = qi - kj
    n = np.maximum(dist, 0)
    max_exact = N_BUCKETS // 2
    nf = np.maximum(n, 1).astype(np.float32)
    large = max_exact + (np.log(nf / np.float32(max_exact)) / np.float32(math.log(MAX_DISTANCE / max_exact))
                         * np.float32(N_BUCKETS - max_exact)).astype(np.int32)
    large = np.minimum(large, N_BUCKETS - 1)
    bucket = np.where(n < max_exact, n, large)
    valid = (dist >= 0) & (dist < WINDOW)
    return np.where(valid, bucket, -1).astype(np.int32)


def _rmsnorm(x, g):
    ms = jnp.mean(x * x, axis=-1)
    r = lax.rsqrt(ms.reshape(-1, LANES) + EPS).reshape(-1, 1)
    return x * r * g


def _prep_kernel(relb_ref, poolw_ref, bkt_ref, *refs):
    n_w = (len(refs) - 2) // 2
    w_f32, (bias_ref, wbd_ref), w_bf16 = refs[:n_w], refs[n_w:n_w + 2], refs[n_w + 2:]
    for src, dst in zip(w_f32, w_bf16):
        dst[...] = src[...].astype(_BF16)

    @pl.when(pl.program_id(0) == 0)
    def _():
        _build_tables(relb_ref, poolw_ref, bkt_ref, bias_ref, wbd_ref)


def _build_tables(relb_ref, poolw_ref, bkt_ref, bias_ref, wbd_ref):
    bkt = bkt_ref[...]
    for h in range(N_Q_HEADS):
        acc = jnp.where(bkt < 0, NEG, 0.0).astype(_F32)
        for b in range(N_BUCKETS):
            acc = jnp.where(bkt == b, relb_ref[b, h] * LOG2E, acc)
        g, gi = divmod(h, GROUP)
        bias_ref[g, :, gi * BLOCK:(gi + 1) * BLOCK] = acc
    rows = lax.broadcasted_iota(jnp.int32, (POOL_GROUP, POOL_WIDTH), 0)
    cols = lax.broadcasted_iota(jnp.int32, (POOL_GROUP, POOL_WIDTH), 1)
    rep = jnp.where((cols % POOL_GROUP) == rows, 1.0, 0.0).astype(_BF16)
    for l in range(poolw_ref.shape[0]):
        for g in range(len(POOL_WINDOWS)):
            tiled = jnp.dot(poolw_ref[l, g].astype(_BF16), rep, preferred_element_type=_F32)
            wbd_ref[l, g * POOL_GROUP:(g + 1) * POOL_GROUP, :] = jnp.where(cols // POOL_GROUP == g, tiled,
                                                                          0.0).astype(_BF16)


def _prep_call(relb, poolw, bkt, weights_f32):
    depth = poolw.shape[0]

    def whole(a):
        return pl.BlockSpec(a.shape, lambda i: (0,) * len(a.shape))

    table_shapes = [jax.ShapeDtypeStruct((N_KV_HEADS, 2 * BLOCK, GROUP * BLOCK), _F32),
                    jax.ShapeDtypeStruct((depth, POOL_WIDTH, POOL_WIDTH), _BF16)]
    rows = [w.shape[1] // PREP_STEPS for w in weights_f32]
    assert all(r * PREP_STEPS == w.shape[1] and r % BF16_SUBLANES == 0 for r, w in zip(rows, weights_f32))
    outs = pl.pallas_call(
        _prep_kernel,
        grid=(PREP_STEPS,),
        in_specs=[pl.BlockSpec(memory_space=pltpu.SMEM), whole(poolw), whole(bkt)]
        + [pl.BlockSpec((None, r, w.shape[2]), lambda i: (0, i, 0)) for r, w in zip(rows, weights_f32)],
        out_specs=[whole(s) for s in table_shapes]
        + [pl.BlockSpec((r, w.shape[2]), lambda i: (i, 0)) for r, w in zip(rows, weights_f32)],
        out_shape=table_shapes + [jax.ShapeDtypeStruct(w.shape[1:], _BF16) for w in weights_f32],
        compiler_params=pltpu.CompilerParams(dimension_semantics=("arbitrary",),
                                             vmem_limit_bytes=V7X_VMEM_LIMIT_BYTES),
        name="hybrid_prep",
    )(relb, poolw, bkt, *weights_f32)
    return outs[0], outs[1], outs[2:]


def _attention_scores(n, g, first_tile, q_scr, k_scr, bias_scr, sink_row):
    r0 = n * BLOCK
    kband = k_scr[r0:r0 + 2 * BLOCK, :]
    s_t = jnp.dot(kband, q_scr[n * N_KV_HEADS + g], preferred_element_type=_F32) + bias_scr[g]
    if n == 0:
        s_t = jnp.concatenate([s_t[:BLOCK] + jnp.where(first_tile, NEG, 0.0), s_t[BLOCK:]], axis=0)
    m = jnp.maximum(jnp.max(s_t, axis=0, keepdims=True), sink_row)
    return jnp.exp2(s_t - m).astype(_BF16), jnp.exp2(sink_row - m)


def _attention_output(n, g, e, e_sink, v_scr, mixed_scr):
    r0 = n * BLOCK
    v_t = v_scr[r0:r0 + 2 * BLOCK, :].T[g * HEAD_DIM:(g + 1) * HEAD_DIM, :]
    v_aug = jnp.concatenate([v_t, jnp.ones((BF16_SUBLANES, 2 * BLOCK), _F32)], axis=0).astype(_BF16)
    o_aug = jnp.dot(v_aug, e, preferred_element_type=_F32)
    inv_denom = 1.0 / (o_aug[HEAD_DIM:HEAD_DIM + 1, :] + e_sink)
    o_t = o_aug[0:HEAD_DIM, :] * inv_denom
    o4 = jnp.concatenate([o_t[:, gi * BLOCK:(gi + 1) * BLOCK] for gi in range(GROUP)], axis=0)
    mixed_scr[r0:r0 + BLOCK, g * GROUP * HEAD_DIM:(g + 1) * GROUP * HEAD_DIM] = o4.T.astype(_BF16)


def _mlp_half(x1_ref, h2_ref, w1_ref, w2_ref, fnorm_ref, o_ref, act_scr):
    for lo in range(0, D_FF, FF_PIECE):
        hid = jnp.dot(h2_ref[...], w1_ref[:, lo:lo + FF_PIECE], preferred_element_type=_F32)
        act_scr[:, lo:lo + FF_PIECE] = jnp.square(jnp.maximum(hid, 0.0)).astype(_BF16)
        yield
    halfd = D_MODEL // 2
    for i in range(2):
        cols = slice(i * halfd, (i + 1) * halfd)
        o_ref[0, :, cols] = x1_ref[:, cols] + jnp.dot(act_scr[...], w2_ref[:, cols], preferred_element_type=_F32)
        yield
    if fnorm_ref is not None:
        o_ref[0] = _rmsnorm(o_ref[0], fnorm_ref[...])
    yield


def _mixer_half(x_ref, x1_ref, h2_ref, seq_tile, n1_ref, win_ref, convw_ref, sinks_ref, pscale_ref, wout_ref,
                n2_ref, layer, q_scr, k_scr, v_scr, g_scr, u_scr, p_scr, a1_scr, a2_scr, a3_scr, mixed_scr,
                bias_scr, wbd_scr):
    first_tile = seq_tile == 0

    x = x_ref[0]
    h = _rmsnorm(x, n1_ref[...]).astype(_BF16)
    q = jnp.dot(h, win_ref[:, 0:ATTN_WIDTH], preferred_element_type=_F32) * Q_SCALE
    zeros = jnp.zeros((HEAD_DIM, GROUP * BLOCK), _F32)
    for n in range(TM // BLOCK):
        for g in range(N_KV_HEADS):
            qg_t = q[n * BLOCK:(n + 1) * BLOCK, g * GROUP * HEAD_DIM:(g + 1) * GROUP * HEAD_DIM].T
            qcat = jnp.concatenate([qg_t[gi * HEAD_DIM:(gi + 1) * HEAD_DIM, :] for gi in range(GROUP)], axis=1)
            rhs = jnp.concatenate([qcat, zeros] if g == 0 else [zeros, qcat], axis=0)
            q_scr[n * N_KV_HEADS + g] = rhs.astype(_BF16)
    for r in range(0, TM, TM // 2):
        kv = jnp.dot(h[r:r + TM // 2], win_ref[:, ATTN_WIDTH:_QKV_END], preferred_element_type=_F32)
        k_scr[KV_HALO + r:KV_HALO + r + TM // 2, :] = kv[:, 0:KV_WIDTH].astype(_BF16)
        v_scr[KV_HALO + r:KV_HALO + r + TM // 2, :] = kv[:, KV_WIDTH:2 * KV_WIDTH]
    yield

    lane = lax.broadcasted_iota(jnp.int32, (1, GROUP * BLOCK), 1)
    sink_rows = []
    for g in range(N_KV_HEADS):
        row = jnp.full((1, GROUP * BLOCK), sinks_ref[layer, g * GROUP + GROUP - 1], _F32)
        for gi in range(GROUP - 2, -1, -1):
            row = jnp.where(lane < (gi + 1) * BLOCK, sinks_ref[layer, g * GROUP + gi], row)
        sink_rows.append(row * LOG2E)
    gates_half = _GATES_WIDTH // 2
    n_pairs = (TM // BLOCK) * N_KV_HEADS
    pending = None
    for k in range(n_pairs):
        n, g = divmod(k, N_KV_HEADS)
        if pending is not None:
            _attention_output(*pending, v_scr, mixed_scr)
        pending = (n, g) + _attention_scores(n, g, first_tile, q_scr, k_scr, bias_scr, sink_rows[g])
        yield
        i = k - (n_pairs - 2)
        if i >= 0:
            lo = _QKV_END + i * gates_half
            g_scr[:, i * gates_half:(i + 1) * gates_half] = jnp.dot(h, win_ref[:, lo:lo + gates_half],
                                                                    preferred_element_type=_F32)
            yield
    _attention_output(*pending, v_scr, mixed_scr)
    k_scr[0:KV_HALO, :] = k_scr[TM:TM + KV_HALO, :]
    v_scr[0:KV_HALO, :] = v_scr[TM:TM + KV_HALO, :]

    c0 = CONV_HALO
    u_scr[c0:c0 + TM, :] = g_scr[:, CONV_WIDTH:2 * CONV_WIDTH] * g_scr[:, 2 * CONV_WIDTH:3 * CONV_WIDTH]
    cw = convw_ref[...]
    y = (cw[0:1, :] * u_scr[c0 - 2:c0 - 2 + TM, :] + cw[1:2, :] * u_scr[c0 - 1:c0 - 1 + TM, :]
         + cw[2:3, :] * u_scr[c0:c0 + TM, :])
    mixed_scr[:, ATTN_WIDTH:ATTN_WIDTH + CONV_WIDTH] = (g_scr[:, 0:CONV_WIDTH] * y).astype(_BF16)
    u_scr[0:CONV_HALO, :] = u_scr[TM:TM + CONV_HALO, :]

    p0 = POOL_HALO
    pe = p0 + TM
    p = g_scr[:, 3 * CONV_WIDTH:_GATES_WIDTH]
    p_scr[p0:pe, :] = p
    a1_scr[8:pe, :] = p_scr[8:pe, :] + p_scr[7:pe - 1, :]
    a2_scr[16:pe, :] = a1_scr[16:pe, :] + a1_scr[14:pe - 2, :]
    hi = slice(2 * POOL_GROUP, POOL_WIDTH)
    a3_scr[24:pe, hi] = a2_scr[24:pe, hi] + a2_scr[20:pe - 4, hi]
    s16 = a3_scr[p0:pe, hi] + a3_scr[p0 - 8:pe - 8, hi]
    first_group = lax.broadcasted_iota(jnp.int32, (TM, 2 * POOL_GROUP), 1) < POOL_GROUP
    ssum = jnp.concatenate([jnp.where(first_group, a1_scr[p0:pe, 0:2 * POOL_GROUP], a2_scr[p0:pe, 0:2 * POOL_GROUP]),
                            jnp.where(first_group, a3_scr[p0:pe, hi], s16)], axis=1)
    head = max(POOL_WINDOWS)
    hlane = lax.broadcasted_iota(jnp.int32, (head, POOL_WIDTH), 1)
    win = jnp.where(hlane < POOL_GROUP, POOL_WINDOWS[0],
                    jnp.where(hlane < 2 * POOL_GROUP, POOL_WINDOWS[1],
                              jnp.where(hlane < 3 * POOL_GROUP, POOL_WINDOWS[2], POOL_WINDOWS[3])))
    t = seq_tile * TM + lax.broadcasted_iota(jnp.int32, (head, POOL_WIDTH), 0)
    inv_head = 1.0 / jnp.minimum(t + 1, win).astype(_F32)
    inv_count = jnp.concatenate([inv_head, jnp.broadcast_to(inv_head[head - 1:head, :], (TM - head, POOL_WIDTH))],
                                axis=0)
    pooled = (ssum * inv_count - p).astype(_BF16)
    p_scr[0:POOL_HALO, :] = p_scr[TM:TM + POOL_HALO, :]
    yield
    for r in range(0, TM, TM // 2):
        mixed_pool = jnp.dot(pooled[r:r + TM // 2], wbd_scr[...], preferred_element_type=_F32) * pscale_ref[...]
        mixed_scr[r:r + TM // 2, ATTN_WIDTH + CONV_WIDTH:D_MODEL] = mixed_pool.astype(_BF16)
    yield

    x1 = x + jnp.dot(mixed_scr[...], wout_ref[...], preferred_element_type=_F32)
    x1_ref[...] = x1
    h2_ref[...] = _rmsnorm(x1, n2_ref[...]).astype(_BF16)
    yield


def _step_rows(s, buf):
    rows = buf.shape[0]
    return pl.ds(pl.multiple_of(s * rows, rows), rows)


def _fetch_copies(s, layer, srcs, in_bufs, sems):
    return [pltpu.make_async_copy(src.at[layer + 1, _step_rows(s, buf), :], buf, sems.at[i])
            for i, (src, buf) in enumerate(zip(srcs, in_bufs))]


def _put_copies(s, dsts, out_bufs, sems):
    return [pltpu.make_async_copy(buf, dst.at[_step_rows(s, buf), :], sems.at[i])
            for i, (dst, buf) in enumerate(zip(dsts, out_bufs))]


_N_BASE_SCRATCH = 16


def _layer_kernel(x_hbm, *refs, layer, final, cast_next, tiles_per_seq, n_tiles):
    o_pos = _N_FIXED_INPUTS + (1 if final else 0) + (4 if cast_next else 0)
    before, o_hbm, after, step_ref = refs[:o_pos], refs[o_pos], refs[o_pos + 1:-1], refs[-1]
    step_ref[0] = 0

    def in_tile(s):
        t = jnp.minimum(s, n_tiles - 1)
        return (t // tiles_per_seq, t % tiles_per_seq, 0)

    def out_tile(s):
        t = jnp.maximum(s - 1, 0)
        return (t // tiles_per_seq, t % tiles_per_seq, 0)

    def step(x_ref, o_ref):
        s = step_ref[0]
        step_ref[0] = s + 1
        _layer_step(s, x_ref, *before, o_ref, *after, layer=layer, final=final, cast_next=cast_next,
                    tiles_per_seq=tiles_per_seq, n_tiles=n_tiles)

    pltpu.emit_pipeline(step, grid=(n_tiles + 1,),
                        in_specs=[pl.BlockSpec((1, TM, D_MODEL), in_tile)],
                        out_specs=[pl.BlockSpec((1, TM, D_MODEL), out_tile)])(x_hbm, o_hbm)


_N_FIXED_INPUTS = 11


def _layer_step(s, x_ref, n1_ref, win_ref, convw_all, sinks_ref, bias_scr, wbd_all, pscale_ref, wout_ref,
                n2_ref, w1_hbm, w2_hbm, *rest, layer, final, cast_next, tiles_per_seq, n_tiles):
    rest = list(rest)
    convw_ref, wbd_scr = convw_all.at[layer], wbd_all.at[layer]
    fnorm_ref = rest.pop(0) if final else None
    next_f32 = [rest.pop(0) for _ in range(4)] if cast_next else []
    o_ref = rest.pop(0)
    next_bf16 = [rest.pop(0) for _ in range(4)] if cast_next else []
    (x1_scr, h2_scr, act_scr, q_scr, k_scr, v_scr, g_scr, u_scr, p_scr, a1_scr, a2_scr, a3_scr, mixed_scr,
     w1_ref, w2_ref, mlp_w_sems) = rest[:_N_BASE_SCRATCH]
    slot = s % 2
    seq_tile = jnp.minimum(s, n_tiles - 1) % tiles_per_seq

    mlp_w_copies = [pltpu.make_async_copy(src, dst, mlp_w_sems.at[i])
                    for i, (src, dst) in enumerate(((w1_hbm, w1_ref), (w2_hbm, w2_ref)))]

    @pl.when(s == 0)
    def _():
        for copy in mlp_w_copies:
            copy.start()

    @pl.when(s == 1)
    def _():
        for copy in mlp_w_copies:
            copy.wait()

    if cast_next:
        cast_scratch = rest[_N_BASE_SCRATCH:]
        in_bufs, out_bufs, (in_sems, out_sems) = cast_scratch[0:4], cast_scratch[4:8], cast_scratch[8:10]

        @pl.when(s < n_tiles)
        def _():
            for fetch in _fetch_copies(s, layer, next_f32, in_bufs, in_sems):
                fetch.start(priority=1)

    @pl.when(jnp.logical_and(seq_tile == 0, s < n_tiles))
    def _():
        k_scr[0:KV_HALO, :] = jnp.zeros((KV_HALO, KV_WIDTH), _BF16)
        v_scr[0:KV_HALO, :] = jnp.zeros((KV_HALO, KV_WIDTH), _F32)
        u_scr[0:CONV_HALO, :] = jnp.zeros((CONV_HALO, CONV_WIDTH), _F32)
        p_scr[0:POOL_HALO, :] = jnp.zeros((POOL_HALO, POOL_WIDTH), _F32)

    def mlp_half():
        return _mlp_half(x1_scr.at[1 - slot], h2_scr.at[1 - slot], w1_ref, w2_ref, fnorm_ref if final else None,
                         o_ref, act_scr)

    def mixer_half():
        this_layer = pl.ds(layer, 1)
        return _mixer_half(x_ref, x1_scr.at[slot], h2_scr.at[slot], seq_tile, n1_ref.at[this_layer], win_ref,
                           convw_ref, sinks_ref, pscale_ref.at[this_layer], wout_ref, n2_ref.at[this_layer], layer,
                           q_scr, k_scr, v_scr, g_scr, u_scr, p_scr, a1_scr, a2_scr, a3_scr, mixed_scr,
                           bias_scr, wbd_scr)

    @pl.when(s == 0)
    def _():
        for _ in mixer_half():
            pass

    @pl.when(jnp.logical_and(s > 0, s < n_tiles))
    def _():
        mlp, mixer = mlp_half(), mixer_half()
        for who in _PHASE_ORDER:
            next(mlp if who == "m" else mixer)
        for half in (mlp, mixer):
            assert next(half, "done") == "done", "phase order does not cover every phase"

    @pl.when(s == n_tiles)
    def _():
        for _ in mlp_half():
            pass

    if cast_next:
        @pl.when(s < n_tiles)
        def _():
            @pl.when(s > 0)
            def _():
                for put in _put_copies(s - 1, next_bf16, out_bufs, out_sems):
                    put.wait()

            for fetch, put, ibuf, obuf in zip(_fetch_copies(s, layer, next_f32, in_bufs, in_sems),
                                              _put_copies(s, next_bf16, out_bufs, out_sems), in_bufs, out_bufs):
                fetch.wait()
                obuf[...] = ibuf[...].astype(_BF16)
                put.start(priority=1)

        @pl.when(s == n_tiles)
        def _():
            for put in _put_copies(s - 1, next_bf16, out_bufs, out_sems):
                put.wait()


def _layer_call(x, layer, n1, win, convw, sinks, bias, wbd, pscale, wout, n2, w1, w2, fnorm, next_f32):
    batch, seq, _ = x.shape
    assert seq % TM == 0 and TM % BLOCK == 0
    tiles_per_seq = seq // TM
    n_tiles = batch * tiles_per_seq
    final = fnorm is not None
    cast_next = next_f32 is not None
    smem = pl.BlockSpec(memory_space=pltpu.SMEM)
    hbm = pl.BlockSpec(memory_space=pl.ANY)

    vmem = pl.BlockSpec(memory_space=pltpu.VMEM)
    in_specs = [hbm, vmem, vmem, vmem, smem, vmem, vmem, vmem, vmem, vmem, hbm, hbm]
    args = [x, n1, win, convw, sinks, bias, wbd, pscale, wout, n2, w1, w2]
    assert len(args) == _N_FIXED_INPUTS + 1
    if final:
        in_specs.append(vmem)
        args.append(fnorm)
    out_specs = [hbm]
    out_shape = [jax.ShapeDtypeStruct(x.shape, x.dtype)]
    cast_scratch = []
    if cast_next:
        in_specs += [hbm] * len(next_f32)
        args += list(next_f32)
        out_specs += [hbm] * len(next_f32)
        out_shape += [jax.ShapeDtypeStruct(w.shape[1:], _BF16) for w in next_f32]
        rows = [w.shape[1] // n_tiles for w in next_f32]
        assert all(r * n_tiles == w.shape[1] and r % BF16_SUBLANES == 0 for r, w in zip(rows, next_f32))
        cast_scratch = ([pltpu.VMEM((r, w.shape[2]), _F32) for r, w in zip(rows, next_f32)]
                        + [pltpu.VMEM((r, w.shape[2]), _BF16) for r, w in zip(rows, next_f32)]
                        + [pltpu.SemaphoreType.DMA((len(next_f32),))] * 2)
    scratch = [
        pltpu.VMEM((2, TM, D_MODEL), _F32),
        pltpu.VMEM((2, TM, D_MODEL), _BF16),
        pltpu.VMEM((TM, D_FF), _BF16),
        pltpu.VMEM(((TM // BLOCK) * N_KV_HEADS, KV_WIDTH, GROUP * BLOCK), _BF16),
        pltpu.VMEM((KV_HALO + TM, KV_WIDTH), _BF16),
        pltpu.VMEM((KV_HALO + TM, KV_WIDTH), _F32),
        pltpu.VMEM((TM, _GATES_WIDTH), _F32),
        pltpu.VMEM((CONV_HALO + TM, CONV_WIDTH), _F32),
        pltpu.VMEM((POOL_HALO + TM, POOL_WIDTH), _F32),
        pltpu.VMEM((POOL_HALO + TM, POOL_WIDTH), _F32),
        pltpu.VMEM((POOL_HALO + TM, POOL_WIDTH), _F32),
        pltpu.VMEM((POOL_HALO + TM, POOL_WIDTH), _F32),
        pltpu.VMEM((TM, D_MODEL), _BF16),
        pltpu.VMEM((D_MODEL, D_FF), _BF16),
        pltpu.VMEM((D_FF, D_MODEL), _BF16),
        pltpu.SemaphoreType.DMA((2,)),
    ]
    assert len(scratch) == _N_BASE_SCRATCH
    scratch += cast_scratch
    scratch.append(pltpu.SMEM((1,), jnp.int32))
    outs = pl.pallas_call(
        functools.partial(_layer_kernel, layer=layer, final=final, cast_next=cast_next,
                          tiles_per_seq=tiles_per_seq, n_tiles=n_tiles),
        in_specs=in_specs,
        out_specs=out_specs,
        out_shape=out_shape,
        scratch_shapes=scratch,
        compiler_params=pltpu.CompilerParams(vmem_limit_bytes=V7X_VMEM_LIMIT_BYTES),
        name="hybrid_layer_final" if final else "hybrid_layer",
    )(*args)
    return outs[0], outs[1:]


def kernel(x, norm1, w_in, conv_w, sinks, pool_w, pool_scale, w_out, norm2, w1, w2, rel_bias, final_norm):
    depth = w_in.shape[0]
    bkt = jnp.asarray(_bucket_table())
    weights_f32 = (w_in, w_out, w1, w2)
    bias, wbd, (win, wout, w1b, w2b) = _prep_call(rel_bias, pool_w, bkt, weights_f32)
    for l in range(depth):
        last = l == depth - 1
        fnorm = final_norm.reshape(1, D_MODEL) if last else None
        x, nxt = _layer_call(x, l, norm1, win, conv_w, sinks, bias, wbd, pool_scale, wout, norm2, w1b, w2b, fnorm,
                             None if last else weights_f32)
        if not last:
            win, wout, w1b, w2b = nxt
    return x
```
